```python
import jax, jax.numpy as jnp
from jax import lax
import numpy as np

D_MODEL = 1024
BATCH = 4
SEQ = 4096
DEPTH = 4
DEC_BATCH = 128
DEC_SEQ = 8
PAST_LEN = 2048
PAGE_SIZE = 128

N_EVEN = (DEPTH + 1) // 2
N_ODD = DEPTH // 2
W_A = D_MODEL
CHUNK = 128
H_A = 8
CH_A = W_A // H_A
W_B = D_MODEL
H_B = 16
HD = W_B // H_B
DILATION_GROUPS = ((128, 1), (512, 4), (2048, 16))
W_MAX = 2048
ROPE_THETA = 10000.0
QUERY_BLOCK = 128
W_C = 2 * D_MODEL
CONV_W = 3
EPS = 1e-6
NEG = -1e30
EVEN_SPLITS = (W_A, 2 * W_A, 3 * W_A, 3 * W_A + W_B, 3 * W_A + 2 * W_B, 3 * W_A + 3 * W_B)
EVEN_IN = 3 * W_A + 4 * W_B
ODD_SPLITS = (W_C, 2 * W_C, 3 * W_C)
ODD_IN = 4 * W_C

kernel_name = 'hybrid_gmlp_dilated_shortconv_step'


def rmsnorm(x, g):
    xf = x.astype(jnp.float32)
    y = xf * lax.rsqrt(jnp.mean(xf * xf, axis=-1, keepdims=True) + EPS)
    return (y * g.astype(jnp.float32)).astype(x.dtype)


def rotary(x, pos):
    half = HD // 2
    inv = 1.0 / (ROPE_THETA ** (jnp.arange(half, dtype=jnp.float32) / half))
    ang = pos.astype(jnp.float32)[:, None] * inv[None, :]
    cos = jnp.cos(ang)[None, :, None, :]
    sin = jnp.sin(ang)[None, :, None, :]
    xf = x.astype(jnp.float32)
    x1, x2 = xf[..., :half], xf[..., half:]
    return jnp.concatenate([x1 * cos - x2 * sin, x2 * cos + x1 * sin], axis=-1).astype(x.dtype)


def chunk_spatial_gate(u, v, w_s, b_s):
    bsz, t = v.shape[0], v.shape[1]
    L = min(t, CHUNK)
    nc = t // L
    mask = jnp.tril(jnp.ones((L, L), dtype=bool))
    w = jnp.where(mask[None], w_s[:, :L, :L], 0.0).astype(v.dtype)
    vc = v.reshape(bsz, nc, L, H_A, CH_A)
    mix = jnp.einsum('hts,bcshd->bcthd', w, vc) + b_s[:, :L].T[None, None, :, :, None].astype(v.dtype)
    return u * mix.reshape(bsz, t, H_A, CH_A)


def dilated_window_attention(q, k_all, v_all, q_start):
    bsz, nq = q.shape[0], q.shape[1]
    qb = QUERY_BLOCK if nq % QUERY_BLOCK == 0 else nq
    nblk = nq // qb
    q_blocks = q.reshape(bsz, nblk, qb, H_B, HD).transpose(1, 0, 2, 3, 4)

    def block(args):
        q_blk, blk = args
        q_idx = q_start + blk * qb + jnp.arange(qb)
        lses, outs = [], []
        for window, dil in DILATION_GROUPS:
            offs = dil * jnp.arange(window // dil + 1)
            idx = q_idx[:, None] - offs[None, :]
            valid = idx >= 0
            idx = jnp.maximum(idx, 0)
            k_g = jnp.take(k_all, idx, axis=1)
            v_g = jnp.take(v_all, idx, axis=1)
            logits = jnp.einsum('bqhd,bqkhd->bhqk', q_blk, k_g).astype(jnp.float32)
            logits = jnp.where(valid[None, None], logits, NEG)
            m = jnp.max(logits, axis=-1)
            p = jnp.exp(logits - m[..., None])
            s = jnp.sum(p, axis=-1)
            o = jnp.einsum('bhqk,bqkhd->bqhd', p.astype(v_g.dtype), v_g).astype(jnp.float32)
            outs.append(o / s.transpose(0, 2, 1)[..., None])
            lses.append(m + jnp.log(s))
        wts = jax.nn.softmax(jnp.stack(lses, axis=0), axis=0)
        wts = wts.transpose(0, 1, 3, 2)[..., None]
        return jnp.sum(wts * jnp.stack(outs, axis=0), axis=0).astype(q_blk.dtype)

    out = lax.map(block, (q_blocks, jnp.arange(nblk)))
    return out.transpose(1, 0, 2, 3, 4).reshape(bsz, nq, H_B, HD)


def even_layer(h, w_in, w_s, b_s, w_out, pos, k_past, v_past):
    bsz, t = h.shape[0], h.shape[1]
    proj = h @ w_in
    u, v, z_a, q, k, vb, z_b = jnp.split(proj, EVEN_SPLITS, axis=-1)
    u = jax.nn.gelu(u, approximate=False)
    v = jax.nn.gelu(v, approximate=False)
    a = chunk_spatial_gate(u.reshape(bsz, t, H_A, CH_A), v.reshape(bsz, t, H_A, CH_A), w_s, b_s)
    a_out = a.reshape(bsz, t, W_A) * jax.nn.silu(z_a)
    q = rotary(q.reshape(bsz, t, H_B, HD), pos) * (HD ** -0.5)
    k = rotary(k.reshape(bsz, t, H_B, HD), pos)
    vb = vb.reshape(bsz, t, H_B, HD)
    if k_past is None:
        k_all, v_all, q_start = k, vb, 0
    else:
        k_all = jnp.concatenate([k_past.astype(k.dtype), k], axis=1)
        v_all = jnp.concatenate([v_past.astype(vb.dtype), vb], axis=1)
        q_start = k_past.shape[1]
    att = dilated_window_attention(q, k_all, v_all, q_start).reshape(bsz, t, W_B)
    b_out = att * jax.nn.silu(z_b)
    out = jnp.concatenate([a_out, b_out], axis=-1) @ w_out
    return out, v, k, vb


def odd_layer(h, w_in, conv_w, w_out, conv_past):
    proj = h @ w_in
    gate_b, gate_c, hx, z = jnp.split(proj, ODD_SPLITS, axis=-1)
    pre = gate_c * hx
    padded = jnp.concatenate([conv_past.astype(pre.dtype), pre], axis=1)
    kern = conv_w[:, None, :].astype(pre.dtype)
    conv = lax.conv_general_dilated(padded, kern, (1,), 'VALID',
                                    dimension_numbers=('NWC', 'WIO', 'NWC'),
                                    feature_group_count=W_C)
    y = gate_b * conv * jax.nn.silu(z)
    return y @ w_out, padded[:, -(CONV_W - 1):]


def setup_inputs(seed: int = 0) -> dict:
    key = jax.random.key(seed)
    ks = jax.random.split(key, 14)
    buf = min(W_MAX, PAST_LEN)

    def nrm(k, shape, scale):
        return jax.random.normal(k, shape, jnp.float32) * scale

    return {
        'x_prompt': nrm(ks[0], (BATCH, SEQ, D_MODEL), 1.0),
        'x_sample': nrm(ks[1], (DEC_BATCH, DEC_SEQ, D_MODEL), 1.0),
        'cache_b_k': nrm(ks[2], (N_EVEN, DEC_BATCH, buf, H_B, HD), 1.0),
        'cache_b_v': nrm(ks[3], (N_EVEN, DEC_BATCH, buf, H_B, HD), 1.0),
        'state_c_conv': nrm(ks[4], (N_ODD, DEC_BATCH, CONV_W - 1, W_C), 1.0),
        'norm_w': 1.0 + nrm(ks[5], (DEPTH, D_MODEL), 0.02),
        'final_norm_w': 1.0 + nrm(ks[6], (D_MODEL,), 0.02),
        'w_in_even': nrm(ks[7], (N_EVEN, D_MODEL, EVEN_IN), D_MODEL ** -0.5),
        'w_s': nrm(ks[8], (N_EVEN, H_A, CHUNK, CHUNK), CHUNK ** -0.5),
        'b_s': 1.0 + nrm(ks[9], (N_EVEN, H_A, CHUNK), 0.02),
        'w_out_even': nrm(ks[10], (N_EVEN, W_A + W_B, D_MODEL), (W_A + W_B) ** -0.5),
        'w_in_odd': nrm(ks[11], (N_ODD, D_MODEL, ODD_IN), D_MODEL ** -0.5),
        'conv_w': nrm(ks[12], (N_ODD, CONV_W, W_C), CONV_W ** -0.5),
        'w_out_odd': nrm(ks[13], (N_ODD, W_C, D_MODEL), W_C ** -0.5),
    }


def reference(x_prompt, x_sample, cache_b_k, cache_b_v, state_c_conv, norm_w, final_norm_w,
              w_in_even, w_s, b_s, w_out_even, w_in_odd, conv_w, w_out_odd):
    seq_p = x_prompt.shape[1]
    seq_s = x_sample.shape[1]
    pos_p = jnp.arange(seq_p)
    pos_s = PAST_LEN + jnp.arange(seq_s)
    buf_p = min(W_MAX, seq_p)
    hp, hs = x_prompt, x_sample
    kp_l, vp_l, ks_l, vs_l, av_l, cp_l, cs_l = [], [], [], [], [], [], []
    for layer in range(DEPTH):
        n_p = rmsnorm(hp, norm_w[layer])
        n_s = rmsnorm(hs, norm_w[layer])
        if layer % 2 == 0:
            e = layer // 2
            o_p, _, k_p, v_p = even_layer(n_p, w_in_even[e], w_s[e], b_s[e], w_out_even[e],
                                          pos_p, None, None)
            o_s, av_s, k_s, v_s = even_layer(n_s, w_in_even[e], w_s[e], b_s[e], w_out_even[e],
                                             pos_s, cache_b_k[e], cache_b_v[e])
            kp_l.append(k_p[:, seq_p - buf_p:])
            vp_l.append(v_p[:, seq_p - buf_p:])
            ks_l.append(k_s)
            vs_l.append(v_s)
            av_l.append(av_s)
        else:
            c = layer // 2
            zeros = jnp.zeros((hp.shape[0], CONV_W - 1, W_C), hp.dtype)
            o_p, c_p = odd_layer(n_p, w_in_odd[c], conv_w[c], w_out_odd[c], zeros)
            o_s, c_s = odd_layer(n_s, w_in_odd[c], conv_w[c], w_out_odd[c], state_c_conv[c])
            cp_l.append(c_p)
            cs_l.append(c_s)
        hp = hp + o_p
        hs = hs + o_s
    y_prompt = rmsnorm(hp, final_norm_w)
    y_sample = rmsnorm(hs, final_norm_w)
    new_b_k_prompt = jnp.stack(kp_l, axis=0)
    new_b_v_prompt = jnp.stack(vp_l, axis=0)
    new_b_k_sample = jnp.stack(ks_l, axis=0)
    new_b_v_sample = jnp.stack(vs_l, axis=0)
    new_a_v_sample = jnp.stack(av_l, axis=0)
    new_c_conv_prompt = jnp.stack(cp_l, axis=0)
    new_c_conv_sample = jnp.stack(cs_l, axis=0)
    return (y_prompt, y_sample, new_b_k_prompt, new_b_v_prompt, new_b_k_sample, new_b_v_sample,
            new_a_v_sample, new_c_conv_prompt, new_c_conv_sample)
```

```python
import functools

import numpy as np
import jax
import jax.numpy as jnp
from jax import lax
from jax.experimental import pallas as pl
from jax.experimental.pallas import tpu as pltpu

D_MODEL = 1024
BATCH = 4
SEQ = 4096
DEPTH = 4
DEC_BATCH = 128
DEC_SEQ = 8
PAST_LEN = 2048
N_EVEN = (DEPTH + 1) // 2
N_ODD = DEPTH // 2
W_A = D_MODEL
CHUNK = 128
H_A = 8
CH_A = W_A // H_A
W_B = D_MODEL
H_B = 16
HD = W_B // H_B
DILATION_GROUPS = ((128, 1), (512, 4), (2048, 16))
W_MAX = 2048
ROPE_THETA = 10000.0
W_C = 2 * D_MODEL
CONV_W = 3
EPS = 1e-6
NEG = -1e30
EVEN_IN = 3 * W_A + 4 * W_B
ODD_IN = 4 * W_C

LANES = 128
SUBLANES = 8
ROWS_P = BATCH * SEQ
ROWS_S = DEC_BATCH * DEC_SEQ
ROWS = ROWS_P + ROWS_S
TM = 512
N_TILES = ROWS // TM
N_TILES_P = ROWS_P // TM
TILES_PER_SEQ = SEQ // TM
QB = 128
CACHE_LEN = min(W_MAX, PAST_LEN)
TAIL = 512
RES = 16
N_RES = CACHE_LEN // RES
KEYS_S = N_RES * DEC_SEQ + TAIL + QB
VMEM_LIMIT = 48 * 1024 * 1024

F32 = jnp.float32
BF16 = jnp.bfloat16


def _dot(a, b):
    return jnp.dot(a, b, preferred_element_type=F32)


def _dot_nt(a, b):
    return lax.dot_general(a, b, (((1,), (1,)), ((), ())), preferred_element_type=F32)


def _params(*sem):
    return pltpu.CompilerParams(dimension_semantics=sem, vmem_limit_bytes=VMEM_LIMIT)


def _rmsnorm_kernel(x_ref, g_ref, o_ref):
    x = x_ref[...]
    ms = jnp.mean(x * x, axis=-1, keepdims=True)
    o_ref[...] = (x * lax.rsqrt(ms + EPS) * g_ref[...]).astype(o_ref.dtype)


def _rmsnorm(x, g):
    return pl.pallas_call(
        _rmsnorm_kernel,
        grid=(N_TILES,),
        in_specs=[pl.BlockSpec((TM, D_MODEL), lambda i: (i, 0)),
                  pl.BlockSpec((1, D_MODEL), lambda i: (0, 0))],
        out_specs=pl.BlockSpec((TM, D_MODEL), lambda i: (i, 0)),
        out_shape=jax.ShapeDtypeStruct((ROWS, D_MODEL), BF16),
        compiler_params=_params("arbitrary"),
        name="rmsnorm",
    )(x, g.reshape(1, D_MODEL))


def _gelu(x):
    return 0.5 * x * (1.0 + lax.erf(x * np.float32(np.sqrt(0.5))))


def _silu(x):
    return x * jax.nn.sigmoid(x)


def _rope(x, cos_ref, sin_ref, o_ref, scale):
    lane = lax.broadcasted_iota(jnp.int32, (TM, LANES), 1)
    first_half = (lane % HD) < (HD // 2)
    cos = cos_ref[...]
    sin = sin_ref[...]
    for c in range(W_B // LANES):
        xc = x[:, c * LANES:(c + 1) * LANES]
        partner = jnp.where(first_half, pltpu.roll(xc, LANES - HD // 2, 1), pltpu.roll(xc, HD // 2, 1))
        out = xc * cos + partner * sin
        if scale is not None:
            out = out * scale
        o_ref[:, c * LANES:(c + 1) * LANES] = out


def _even_in_kernel(x_ref, w_ref, cos_ref, sin_ref, o_ref):
    j = pl.program_id(1)
    acc = _dot(x_ref[...], w_ref[...])

    @pl.when(j <= 1)
    def _():
        o_ref[...] = _gelu(acc)

    @pl.when((j == 2) | (j == 6))
    def _():
        o_ref[...] = _silu(acc)

    @pl.when(j == 3)
    def _():
        _rope(acc, cos_ref, sin_ref, o_ref, np.float32(HD ** -0.5))

    @pl.when(j == 4)
    def _():
        _rope(acc, cos_ref, sin_ref, o_ref, None)

    @pl.when(j == 5)
    def _():
        o_ref[...] = acc


def _table_block(i):
    return jnp.where(i < N_TILES_P, i % TILES_PER_SEQ, TILES_PER_SEQ)


def _even_in(n, w, cos, sin):
    return pl.pallas_call(
        _even_in_kernel,
        grid=(N_TILES, EVEN_IN // D_MODEL),
        in_specs=[pl.BlockSpec((TM, D_MODEL), lambda i, j: (i, 0)),
                  pl.BlockSpec((D_MODEL, D_MODEL), lambda i, j: (0, j)),
                  pl.BlockSpec((TM, LANES), lambda i, j: (_table_block(i), 0)),
                  pl.BlockSpec((TM, LANES), lambda i, j: (_table_block(i), 0))],
        out_specs=pl.BlockSpec((TM, D_MODEL), lambda i, j: (i, j)),
        out_shape=jax.ShapeDtypeStruct((ROWS, EVEN_IN), F32),
        compiler_params=_params("arbitrary", "arbitrary"),
        name="even_in",
    )(n, w, cos, sin)


def _gmlp_kernel(u_ref, v_ref, z_ref, wm_ref, bias_ref, o_ref):
    for h in range(H_A):
        sl = slice(h * CH_A, (h + 1) * CH_A)
        mix = _dot(wm_ref[h], v_ref[:, sl].astype(BF16)) + bias_ref[:, sl]
        o_ref[:, sl] = (u_ref[:, sl] * mix * z_ref[:, sl]).astype(o_ref.dtype)


def _gmlp(proj, wm, bias):
    n_chunks = ROWS // CHUNK
    n_chunks_p = ROWS_P // CHUNK
    variant = lambda i: jnp.where(i < n_chunks_p, 0, 1)
    return pl.pallas_call(
        _gmlp_kernel,
        grid=(n_chunks,),
        in_specs=[pl.BlockSpec((CHUNK, W_A), lambda i: (i, 0)),
                  pl.BlockSpec((CHUNK, W_A), lambda i: (i, 1)),
                  pl.BlockSpec((CHUNK, W_A), lambda i: (i, 2)),
                  pl.BlockSpec((None, H_A, CHUNK, CHUNK), lambda i: (variant(i), 0, 0, 0)),
                  pl.BlockSpec((None, CHUNK, W_A), lambda i: (variant(i), 0, 0))],
        out_specs=pl.BlockSpec((CHUNK, W_A), lambda i: (i, 0)),
        out_shape=jax.ShapeDtypeStruct((ROWS, W_A), BF16),
        compiler_params=_params("arbitrary"),
        name="gmlp",
    )(proj, proj, proj, wm, bias)


def _attn_prompt_kernel(q_ref, k_ref, v_ref, z_ref, o_ref, acc_ref, m0_ref, m1_ref, l0_ref, l1_ref):
    lane = lax.broadcasted_iota(jnp.int32, (QB, LANES), 1)
    row = lax.broadcasted_iota(jnp.int32, (QB, LANES), 0)
    head0 = lane < HD
    diff = lane - row
    m_refs = (m0_ref, m1_ref)
    l_refs = (l0_ref, l1_ref)

    for g, (_, dil) in enumerate(DILATION_GROUPS):
        n_blk = SEQ // dil // QB

        def rows_at(base, dil=dil):
            if dil == 1:
                return pl.ds(pl.multiple_of(base, QB), QB)
            return pl.ds(base, QB, stride=dil)

        def body(step, carry, g=g, dil=dil, n_blk=n_blk, rows_at=rows_at):
            res = step // n_blk
            blk = step % n_blk
            base = res + dil * QB * blk
            cur = rows_at(base)
            prev = rows_at(jnp.maximum(base - dil * QB, res))
            prev_thr = jnp.where(blk > 0, 0, 2 * QB)
            q = q_ref[cur, :]
            kc = k_ref[cur, :].astype(BF16)
            kp = k_ref[prev, :].astype(BF16)
            vc = v_ref[cur, :].astype(BF16)
            vp = v_ref[prev, :].astype(BF16)
            alphas, pvs = [], []
            for h in range(2):
                hm = head0 if h == 0 else jnp.logical_not(head0)
                qh = jnp.where(hm, q, 0.0).astype(BF16)
                sp = jnp.where(diff >= prev_thr, _dot_nt(qh, kp), NEG)
                sc = jnp.where(diff <= 0, _dot_nt(qh, kc), NEG)
                mt = jnp.maximum(jnp.max(sp, axis=1, keepdims=True), jnp.max(sc, axis=1, keepdims=True))
                if g == 0:
                    m_new = jnp.broadcast_to(mt, (QB, LANES))
                else:
                    m_old = m_refs[h][cur, :]
                    m_new = jnp.maximum(m_old, mt)
                    alpha = jnp.exp(m_old - m_new)
                pp = jnp.exp(sp - m_new)
                pc = jnp.exp(sc - m_new)
                lt = jnp.sum(pp, axis=1, keepdims=True) + jnp.sum(pc, axis=1, keepdims=True)
                if g == 0:
                    l_new = jnp.broadcast_to(lt, (QB, LANES))
                else:
                    l_new = alpha * l_refs[h][cur, :] + lt
                    alphas.append(alpha)
                m_refs[h][cur, :] = m_new
                l_refs[h][cur, :] = l_new
                pvs.append(_dot(pp.astype(BF16), vp) + _dot(pc.astype(BF16), vc))
            pv = jnp.where(head0, pvs[0], pvs[1])
            if g == 0:
                acc_ref[cur, :] = pv
            else:
                acc_ref[cur, :] = jnp.where(head0, alphas[0], alphas[1]) * acc_ref[cur, :] + pv
            return carry

        lax.fori_loop(0, dil * n_blk, body, 0)

    def finish(blk, carry):
        rows = pl.ds(pl.multiple_of(blk * QB, QB), QB)
        l = jnp.where(head0, l0_ref[rows, :], l1_ref[rows, :])
        o_ref[rows, :] = (acc_ref[rows, :] / l * z_ref[rows, :]).astype(o_ref.dtype)
        return carry

    lax.fori_loop(0, SEQ // QB, finish, 0)


def _attn_prompt(proj):
    n_pairs = W_B // LANES
    col = lambda first: (lambda b, p: (b, first * (D_MODEL // LANES) + p))
    return pl.pallas_call(
        _attn_prompt_kernel,
        grid=(BATCH, n_pairs),
        in_specs=[pl.BlockSpec((SEQ, LANES), col(3)),
                  pl.BlockSpec((SEQ, LANES), col(4)),
                  pl.BlockSpec((SEQ, LANES), col(5)),
                  pl.BlockSpec((SEQ, LANES), col(6))],
        out_specs=pl.BlockSpec((SEQ, LANES), lambda b, p: (b, p)),
        out_shape=jax.ShapeDtypeStruct((ROWS, W_B), BF16),
        scratch_shapes=[pltpu.VMEM((SEQ, LANES), F32)] * 5,
        compiler_params=_params("arbitrary", "arbitrary"),
        name="attn_prompt",
    )(proj, proj, proj, proj)


def _sample_key_counts():
    i = (np.arange(H_B * DEC_SEQ) % DEC_SEQ)[:, None]
    slot = np.arange(KEYS_S)[None, :]
    n_res = N_RES * DEC_SEQ
    counts = np.zeros((H_B * DEC_SEQ, KEYS_S), np.float32)
    in_res = slot < n_res
    counts += in_res & ((slot % DEC_SEQ) == i)
    in_tail = (slot >= n_res) & (slot < n_res + TAIL)
    cache_row = CACHE_LEN - TAIL + (slot - n_res)
    q_row = CACHE_LEN + i
    for window, dil in DILATION_GROUPS[:2]:
        dist = q_row - cache_row
        counts += in_tail & (dist >= 0) & (dist <= window) & (dist % dil == 0)
    in_new = (slot >= n_res + TAIL) & (slot < n_res + TAIL + DEC_SEQ)
    dist = i - (slot - n_res - TAIL)
    for window, dil in DILATION_GROUPS:
        counts += in_new & (dist >= 0) & (dist <= window) & (dist % dil == 0)
    return counts


def _attn_sample_kernel(q_ref, kn_ref, vn_ref, z_ref, kt_ref, vt_ref, kr_ref, vr_ref, cnt_ref, b_any, o_ref,
                        k_all, v_all):
    del b_any
    n_res = N_RES * DEC_SEQ
    n_q = H_B * DEC_SEQ
    for src_r, src_t, src_n, dst in ((kr_ref, kt_ref, kn_ref, k_all), (vr_ref, vt_ref, vn_ref, v_all)):
        dst[0:n_res, :] = src_r[...].reshape(n_res, W_B).astype(BF16)
        dst[n_res:n_res + TAIL, :] = src_t[...].astype(BF16)
        new = jnp.concatenate([src_n[...], jnp.zeros((QB - DEC_SEQ, W_B), F32)], axis=0)
        dst[n_res + TAIL:KEYS_S, :] = new.astype(BF16)

    lane = lax.broadcasted_iota(jnp.int32, (n_q, W_B), 1)
    row = lax.broadcasted_iota(jnp.int32, (n_q, W_B), 0)
    own_head = (lane // HD) == (row // DEC_SEQ)
    q_rep = jnp.concatenate([q_ref[...]] * H_B, axis=0)
    q_big = jnp.where(own_head, q_rep, 0.0).astype(BF16)
    cnt = cnt_ref[...]
    s = jnp.where(cnt > 0.0, _dot_nt(q_big, k_all[...]), NEG)
    m = jnp.max(s, axis=1, keepdims=True)
    e = jnp.exp(s - m) * cnt
    l = jnp.sum(e, axis=1, keepdims=True)
    out_big = _dot(e.astype(BF16), v_all[...]) / l
    out_big = jnp.where(own_head, out_big, 0.0)
    out = jnp.sum(out_big.reshape(H_B, DEC_SEQ, W_B), axis=0)
    o_ref[...] = (out * z_ref[...]).astype(o_ref.dtype)


def _attn_sample(proj, cache_k, cache_v, counts, b_buf):
    first = ROWS_P // DEC_SEQ
    col = lambda c: (lambda b: (first + b, c))
    k4 = cache_k.reshape(DEC_BATCH, CACHE_LEN, W_B)
    v4 = cache_v.reshape(DEC_BATCH, CACHE_LEN, W_B)
    k5 = cache_k.reshape(DEC_BATCH, N_RES, RES, W_B)
    v5 = cache_v.reshape(DEC_BATCH, N_RES, RES, W_B)
    tail_spec = pl.BlockSpec((None, TAIL, W_B), lambda b: (b, CACHE_LEN // TAIL - 1, 0))
    res_spec = pl.BlockSpec((None, N_RES, DEC_SEQ, W_B), lambda b: (b, 0, 0, 0))
    return pl.pallas_call(
        _attn_sample_kernel,
        grid=(DEC_BATCH,),
        in_specs=[pl.BlockSpec((DEC_SEQ, W_B), col(3)),
                  pl.BlockSpec((DEC_SEQ, W_B), col(4)),
                  pl.BlockSpec((DEC_SEQ, W_B), col(5)),
                  pl.BlockSpec((DEC_SEQ, W_B), col(6)),
                  tail_spec, tail_spec, res_spec, res_spec,
                  pl.BlockSpec((H_B * DEC_SEQ, KEYS_S), lambda b: (0, 0)),
                  pl.BlockSpec(memory_space=pl.ANY)],
        out_specs=pl.BlockSpec((DEC_SEQ, W_B), lambda b: (first + b, 0)),
        out_shape=jax.ShapeDtypeStruct((ROWS, W_B), BF16),
        scratch_shapes=[pltpu.VMEM((KEYS_S, W_B), BF16)] * 2,
        input_output_aliases={9: 0},
        compiler_params=_params("arbitrary"),
        name="attn_sample",
    )(proj, proj, proj, proj, k4, v4, k5, v5, counts, b_buf)


def _out_proj_kernel(a_ref, b_ref, w_ref, h_ref, g_ref, h_out_ref, n_out_ref):
    half = w_ref.shape[0] // 2
    h_new = h_ref[...] + _dot(a_ref[...], w_ref[0:half, :]) + _dot(b_ref[...], w_ref[half:, :])
    h_out_ref[...] = h_new
    ms = jnp.mean(h_new * h_new, axis=-1, keepdims=True)
    n_out_ref[...] = (h_new * lax.rsqrt(ms + EPS) * g_ref[...]).astype(n_out_ref.dtype)


def _out_proj(a, a_col, b, b_col, w, h, g, norm_dtype):
    half = w.shape[0] // 2
    return pl.pallas_call(
        _out_proj_kernel,
        grid=(N_TILES,),
        in_specs=[pl.BlockSpec((TM, half), lambda i: (i, a_col)),
                  pl.BlockSpec((TM, half), lambda i: (i, b_col)),
                  pl.BlockSpec((2 * half, D_MODEL), lambda i: (0, 0)),
                  pl.BlockSpec((TM, D_MODEL), lambda i: (i, 0)),
                  pl.BlockSpec((1, D_MODEL), lambda i: (0, 0))],
        out_specs=[pl.BlockSpec((TM, D_MODEL), lambda i: (i, 0)),
                   pl.BlockSpec((TM, D_MODEL), lambda i: (i, 0))],
        out_shape=[jax.ShapeDtypeStruct((ROWS, D_MODEL), F32),
                   jax.ShapeDtypeStruct((ROWS, D_MODEL), norm_dtype)],
        compiler_params=_params("arbitrary"),
        name="out_proj",
    )(a, b, w, h, g.reshape(1, D_MODEL))


TN_ODD = 512
N_COL_ODD = W_C // TN_ODD
PRE_SLOTS = BATCH + ROWS_S // TM


def _odd_in_kernel(x_ref, wb_ref, wc_ref, wx_ref, wz_ref, cw_ref, e1_ref, e2_ref, y_ref, pre_ref, buf):
    i = pl.program_id(1)

    @pl.when(i == 0)
    def _():
        buf[0:SUBLANES, :] = jnp.zeros((SUBLANES, TN_ODD), F32)

    x = x_ref[...]
    pre = _dot(x, wc_ref[...]) * _dot(x, wx_ref[...])
    pre_ref[...] = pre
    buf[SUBLANES:SUBLANES + TM, :] = pre
    row = lax.broadcasted_iota(jnp.int32, (TM, TN_ODD), 0)
    t = jnp.where(i < N_TILES_P, (i % TILES_PER_SEQ) * TM + row, row % DEC_SEQ)
    pre_m1 = jnp.where(t >= 1, buf[SUBLANES - 1:SUBLANES - 1 + TM, :], e1_ref[...])
    pre_m2 = jnp.where(t >= 2, buf[SUBLANES - 2:SUBLANES - 2 + TM, :], e2_ref[...])
    conv = cw_ref[0:1, :] * pre_m2 + cw_ref[1:2, :] * pre_m1 + cw_ref[2:3, :] * pre
    y = _dot(x, wb_ref[...]) * conv * _silu(_dot(x, wz_ref[...]))
    y_ref[...] = y.astype(y_ref.dtype)
    buf[0:SUBLANES, :] = pre[TM - SUBLANES:, :]


def _odd_in(n, w, conv_w, e1, e2):
    wcol = lambda part: (lambda c, i: (0, part * N_COL_ODD + c))
    e_block = lambda c, i: (jnp.where(i < N_TILES_P, 0, i - N_TILES_P + 1), c)
    pre_block = lambda c, i: (jnp.where(i < N_TILES_P, i // TILES_PER_SEQ, i - N_TILES_P + BATCH), c)
    return pl.pallas_call(
        _odd_in_kernel,
        grid=(N_COL_ODD, N_TILES),
        in_specs=[pl.BlockSpec((TM, D_MODEL), lambda c, i: (i, 0)),
                  pl.BlockSpec((D_MODEL, TN_ODD), wcol(0)),
                  pl.BlockSpec((D_MODEL, TN_ODD), wcol(1)),
                  pl.BlockSpec((D_MODEL, TN_ODD), wcol(2)),
                  pl.BlockSpec((D_MODEL, TN_ODD), wcol(3)),
                  pl.BlockSpec((CONV_W, TN_ODD), lambda c, i: (0, c)),
                  pl.BlockSpec((TM, TN_ODD), e_block),
                  pl.BlockSpec((TM, TN_ODD), e_block)],
        out_specs=[pl.BlockSpec((TM, TN_ODD), lambda c, i: (i, c)),
                   pl.BlockSpec((TM, TN_ODD), pre_block)],
        out_shape=[jax.ShapeDtypeStruct((ROWS, W_C), BF16),
                   jax.ShapeDtypeStruct((PRE_SLOTS * TM, W_C), F32)],
        scratch_shapes=[pltpu.VMEM((SUBLANES + TM, TN_ODD), F32)],
        compiler_params=_params("arbitrary", "arbitrary"),
        name="odd_in",
    )(n, w, w, w, w, conv_w, e1, e2)


def _rope_tables():
    half = HD // 2
    inv = 1.0 / (ROPE_THETA ** (jnp.arange(half, dtype=F32) / half))
    pos_s = PAST_LEN + (jnp.arange(TM) % DEC_SEQ)
    pos = jnp.concatenate([jnp.arange(SEQ), pos_s]).astype(F32)
    ang = pos[:, None] * inv[None, :]
    cos = jnp.cos(ang)
    sin = jnp.sin(ang)
    reps = LANES // HD
    return (jnp.tile(jnp.concatenate([cos, cos], axis=1), (1, reps)),
            jnp.tile(jnp.concatenate([-sin, sin], axis=1), (1, reps)))


def _gmlp_weights(w_s, b_s):
    tril = jnp.tril(jnp.ones((CHUNK, CHUNK), bool))
    w_p = jnp.where(tril[None], w_s, 0.0)
    small = w_p[:, :DEC_SEQ, :DEC_SEQ]
    eye = jnp.eye(CHUNK // DEC_SEQ, dtype=F32)
    w_smp = jnp.einsum('ab,hts->hatbs', eye, small).reshape(H_A, CHUNK, CHUNK)
    wm = jnp.stack([w_p, w_smp]).astype(BF16)
    bias_p = jnp.repeat(b_s.T, CH_A, axis=1)
    bias_s = jnp.tile(bias_p[:DEC_SEQ], (CHUNK // DEC_SEQ, 1))
    return wm, jnp.stack([bias_p, bias_s])


def _conv_edges(state):
    e1 = jnp.zeros((DEC_BATCH, DEC_SEQ, W_C), F32).at[:, 0].set(state[:, 1])
    e2 = jnp.zeros((DEC_BATCH, DEC_SEQ, W_C), F32).at[:, 0].set(state[:, 0]).at[:, 1].set(state[:, 1])
    pad = jnp.zeros((TM, W_C), F32)
    return (jnp.concatenate([pad, e1.reshape(ROWS_S, W_C)]), jnp.concatenate([pad, e2.reshape(ROWS_S, W_C)]))


def kernel(x_prompt, x_sample, cache_b_k, cache_b_v, state_c_conv, norm_w, final_norm_w,
           w_in_even, w_s, b_s, w_out_even, w_in_odd, conv_w, w_out_odd):
    h = jnp.concatenate([x_prompt.reshape(ROWS_P, D_MODEL), x_sample.reshape(ROWS_S, D_MODEL)])
    cos, sin = _rope_tables()
    counts = jnp.asarray(_sample_key_counts())
    n = _rmsnorm(h, norm_w[0])
    kp_l, vp_l, ks_l, vs_l, av_l, cp_l, cs_l = [], [], [], [], [], [], []
    buf_p = min(W_MAX, SEQ)
    for layer in range(DEPTH):
        last = layer == DEPTH - 1
        g_next = final_norm_w if last else norm_w[layer + 1]
        norm_dtype = F32 if last else BF16
        if layer % 2 == 0:
            e = layer // 2
            proj = _even_in(n, w_in_even[e].astype(BF16), cos, sin)
            wm, bias = _gmlp_weights(w_s[e], b_s[e])
            a_out = _gmlp(proj, wm, bias)
            b_out = _attn_prompt(proj)
            b_out = _attn_sample(proj, cache_b_k[e], cache_b_v[e], counts, b_out)
            h, n = _out_proj(a_out, 0, b_out, 0, w_out_even[e].astype(BF16), h, g_next, norm_dtype)
            proj_p = proj[:ROWS_P].reshape(BATCH, SEQ, EVEN_IN)
            proj_s = proj[ROWS_P:].reshape(DEC_BATCH, DEC_SEQ, EVEN_IN)
            kp_l.append(proj_p[:, SEQ - buf_p:, 4 * D_MODEL:5 * D_MODEL].reshape(BATCH, buf_p, H_B, HD))
            vp_l.append(proj_p[:, SEQ - buf_p:, 5 * D_MODEL:6 * D_MODEL].reshape(BATCH, buf_p, H_B, HD))
            ks_l.append(proj_s[:, :, 4 * D_MODEL:5 * D_MODEL].reshape(DEC_BATCH, DEC_SEQ, H_B, HD))
            vs_l.append(proj_s[:, :, 5 * D_MODEL:6 * D_MODEL].reshape(DEC_BATCH, DEC_SEQ, H_B, HD))
            av_l.append(proj_s[:, :, D_MODEL:2 * D_MODEL])
        else:
            c = layer // 2
            e1, e2 = _conv_edges(state_c_conv[c])
            y, pre = _odd_in(n, w_in_odd[c].astype(BF16), conv_w[c], e1, e2)
            h, n = _out_proj(y, 0, y, 1, w_out_odd[c].astype(BF16), h, g_next, norm_dtype)
            pre_p = pre[:BATCH * TM].reshape(BATCH, TM, W_C)
            cp_l.append(pre_p[:, TM - (CONV_W - 1):])
            cs_l.append(pre[BATCH * TM:].reshape(DEC_BATCH, DEC_SEQ, W_C)[:, DEC_SEQ - (CONV_W - 1):])
    y_prompt = n[:ROWS_P].reshape(BATCH, SEQ, D_MODEL)
    y_sample = n[ROWS_P:].reshape(DEC_BATCH, DEC_SEQ, D_MODEL)
    return (y_prompt, y_sample, jnp.stack(kp_l), jnp.stack(vp_l), jnp.stack(ks_l), jnp.stack(vs_l),
            jnp.stack(av_l), jnp.stack(cp_l), jnp.stack(cs_l))
```

```python
import numpy as np
import jax
import jax.numpy as jnp
from jax import lax
from jax.experimental import pallas as pl
from jax.experimental.pallas import tpu as pltpu

D_MODEL = 1024
BATCH = 4
SEQ = 4096
DEPTH = 4
DEC_BATCH = 128
DEC_SEQ = 8
PAST_LEN = 2048
N_EVEN = (DEPTH + 1) // 2
N_ODD = DEPTH // 2
W_A = D_MODEL
CHUNK = 128
H_A = 8
CH_A = W_A // H_A
W_B = D_MODEL
H_B = 16
HD = W_B // H_B
DILATION_GROUPS = ((128, 1), (512, 4), (2048, 16))
W_MAX = 2048
ROPE_THETA = 10000.0
W_C = 2 * D_MODEL
CONV_W = 3
EPS = 1e-6
NEG = -1e30
EVEN_IN = 3 * W_A + 4 * W_B
ODD_IN = 4 * W_C

LANES = 128
SUBLANES = 8
ROWS_P = BATCH * SEQ
ROWS_S = DEC_BATCH * DEC_SEQ
ROWS = ROWS_P + ROWS_S
TM = 512
N_TILES = ROWS // TM
N_TILES_P = ROWS_P // TM
TILES_PER_SEQ = SEQ // TM
QB = 128
CACHE_LEN = min(W_MAX, PAST_LEN)
HEADS_PER_STEP = 8
VMEM_LIMIT = 48 * 1024 * 1024

assert all(window // dil == QB for window, dil in DILATION_GROUPS)

F32 = jnp.float32
BF16 = jnp.bfloat16


def _dot(a, b):
    return jnp.dot(a, b, preferred_element_type=F32)


def _dot_nt(a, b):
    return lax.dot_general(a, b, (((1,), (1,)), ((), ())), preferred_element_type=F32)


def _params(*sem):
    return pltpu.CompilerParams(dimension_semantics=sem, vmem_limit_bytes=VMEM_LIMIT)


def _rmsnorm_kernel(x_ref, g_ref, o_ref):
    x = x_ref[...]
    ms = jnp.mean(x * x, axis=-1, keepdims=True)
    o_ref[...] = (x * lax.rsqrt(ms + EPS) * g_ref[...]).astype(o_ref.dtype)


def _rmsnorm(x, g):
    return pl.pallas_call(
        _rmsnorm_kernel,
        grid=(N_TILES,),
        in_specs=[pl.BlockSpec((TM, D_MODEL), lambda i: (i, 0)),
                  pl.BlockSpec((1, D_MODEL), lambda i: (0, 0))],
        out_specs=pl.BlockSpec((TM, D_MODEL), lambda i: (i, 0)),
        out_shape=jax.ShapeDtypeStruct((ROWS, D_MODEL), BF16),
        compiler_params=_params("arbitrary"),
        name="rmsnorm",
    )(x, g.reshape(1, D_MODEL))


def _gelu(x):
    return 0.5 * x * (1.0 + lax.erf(x * np.float32(np.sqrt(0.5))))


def _silu(x):
    return x * jax.nn.sigmoid(x)


def _rope(x, cos_ref, sin_ref, o_ref, scale):
    lane = lax.broadcasted_iota(jnp.int32, (TM, LANES), 1)
    first_half = (lane % HD) < (HD // 2)
    cos = cos_ref[...]
    sin = sin_ref[...]
    for c in range(W_B // LANES):
        xc = x[:, c * LANES:(c + 1) * LANES]
        partner = jnp.where(first_half, pltpu.roll(xc, LANES - HD // 2, 1), pltpu.roll(xc, HD // 2, 1))
        out = xc * cos + partner * sin
        if scale is not None:
            out = out * scale
        o_ref[:, c * LANES:(c + 1) * LANES] = out


def _even_in_kernel(x_ref, w_ref, cos_ref, sin_ref, o_ref):
    j = pl.program_id(1)
    acc = _dot(x_ref[...], w_ref[...])

    @pl.when(j <= 1)
    def _():
        o_ref[...] = _gelu(acc)

    @pl.when((j == 2) | (j == 6))
    def _():
        o_ref[...] = _silu(acc)

    @pl.when(j == 3)
    def _():
        _rope(acc, cos_ref, sin_ref, o_ref, np.float32(HD ** -0.5))

    @pl.when(j == 4)
    def _():
        _rope(acc, cos_ref, sin_ref, o_ref, None)

    @pl.when(j == 5)
    def _():
        o_ref[...] = acc


def _table_block(i):
    return jnp.where(i < N_TILES_P, i % TILES_PER_SEQ, TILES_PER_SEQ)


def _even_in(n, w, cos, sin):
    return pl.pallas_call(
        _even_in_kernel,
        grid=(N_TILES, EVEN_IN // D_MODEL),
        in_specs=[pl.BlockSpec((TM, D_MODEL), lambda i, j: (i, 0)),
                  pl.BlockSpec((D_MODEL, D_MODEL), lambda i, j: (0, j)),
                  pl.BlockSpec((TM, LANES), lambda i, j: (_table_block(i), 0)),
                  pl.BlockSpec((TM, LANES), lambda i, j: (_table_block(i), 0))],
        out_specs=pl.BlockSpec((TM, D_MODEL), lambda i, j: (i, j)),
        out_shape=jax.ShapeDtypeStruct((ROWS, EVEN_IN), F32),
        compiler_params=_params("arbitrary", "arbitrary"),
        name="even_in",
    )(n, w, cos, sin)


def _gmlp_kernel(u_ref, v_ref, z_ref, wm_ref, bias_ref, o_ref):
    for h in range(H_A):
        sl = slice(h * CH_A, (h + 1) * CH_A)
        mix = _dot(wm_ref[h], v_ref[:, sl].astype(BF16)) + bias_ref[:, sl]
        o_ref[:, sl] = (u_ref[:, sl] * mix * z_ref[:, sl]).astype(o_ref.dtype)


def _gmlp(proj, wm, bias):
    n_chunks = ROWS // CHUNK
    n_chunks_p = ROWS_P // CHUNK
    variant = lambda i: jnp.where(i < n_chunks_p, 0, 1)
    return pl.pallas_call(
        _gmlp_kernel,
        grid=(n_chunks,),
        in_specs=[pl.BlockSpec((CHUNK, W_A), lambda i: (i, 0)),
                  pl.BlockSpec((CHUNK, W_A), lambda i: (i, 1)),
                  pl.BlockSpec((CHUNK, W_A), lambda i: (i, 2)),
                  pl.BlockSpec((None, H_A, CHUNK, CHUNK), lambda i: (variant(i), 0, 0, 0)),
                  pl.BlockSpec((None, CHUNK, W_A), lambda i: (variant(i), 0, 0))],
        out_specs=pl.BlockSpec((CHUNK, W_A), lambda i: (i, 0)),
        out_shape=jax.ShapeDtypeStruct((ROWS, W_A), BF16),
        compiler_params=_params("arbitrary"),
        name="gmlp",
    )(proj, proj, proj, wm, bias)


def _band_bias():
    i = np.arange(QB)[:, None]
    j = np.arange(2 * QB)[None, :]
    band = (j - i >= 0) & (j - i <= QB)
    first = band & (j >= QB)
    return np.where(np.stack([band, first]), 0.0, NEG).astype(np.float32)


def _attn_prompt_kernel(q_ref, k_ref, v_ref, z_ref, bias_ref, o_ref,
                        q0_s, q1_s, k_s, v_s, o0_s, o1_s, o2_s, lse0_s, lse1_s, lse2_s):
    o_refs = (o0_s, o1_s, o2_s)
    lse_refs = (lse0_s, lse1_s, lse2_s)
    head0 = lax.broadcasted_iota(jnp.int32, (QB, LANES), 1) < HD
    ones = jnp.ones((2 * QB, LANES), BF16)
    k_s[0:QB, :] = jnp.zeros((QB, LANES), BF16)
    v_s[0:QB, :] = jnp.zeros((QB, LANES), BF16)

    for g, (_, dil) in enumerate(DILATION_GROUPS):
        n = SEQ // dil
        ch = min(n, 4 * QB)
        n_ch = n // ch
        n_blk = n // QB

        def deinterleave(idx, carry, dil=dil, ch=ch, n_ch=n_ch):
            start = idx // n_ch + dil * ch * (idx % n_ch)
            src = pl.ds(pl.multiple_of(start, ch), ch) if dil == 1 else pl.ds(start, ch, stride=dil)
            dst = pl.ds(pl.multiple_of(idx * ch, QB), ch)
            dst_pad = pl.ds(pl.multiple_of(QB + idx * ch, QB), ch)
            first = lax.broadcasted_iota(jnp.int32, (ch, LANES), 1) < HD
            q = q_ref[src, :]
            q0_s[dst, :] = jnp.where(first, q, 0.0).astype(BF16)
            q1_s[dst, :] = jnp.where(first, 0.0, q).astype(BF16)
            k_s[dst_pad, :] = k_ref[src, :].astype(BF16)
            v_s[dst_pad, :] = v_ref[src, :].astype(BF16)
            return carry

        lax.fori_loop(0, SEQ // ch, deinterleave, 0)

        def body(blk, carry, g=g, dil=dil, n_blk=n_blk):
            res = blk // n_blk
            pos = blk % n_blk
            rows = pl.ds(pl.multiple_of(blk * QB, QB), QB)
            rows2 = pl.ds(pl.multiple_of(blk * QB, QB), 2 * QB)
            k2 = k_s[rows2, :]
            v_aug = jnp.concatenate([v_s[rows2, :], ones], axis=1)
            bias = bias_ref[jnp.where(pos > 0, 0, 1)]
            outs, lses = [], []
            for q_s in (q0_s, q1_s):
                s = _dot_nt(q_s[rows, :], k2) + bias
                m = jnp.max(s, axis=1, keepdims=True)
                p = jnp.exp(s - m).astype(BF16)
                r = _dot(p, v_aug)
                l = r[:, LANES:]
                outs.append(r[:, :LANES] / l)
                lses.append(m + jnp.log(l))
            start = res + dil * QB * pos
            dst = pl.ds(pl.multiple_of(start, QB), QB) if dil == 1 else pl.ds(start, QB, stride=dil)
            o_refs[g][dst, :] = jnp.where(head0, outs[0], outs[1])
            lse_refs[g][dst, :] = jnp.where(head0, lses[0], lses[1])
            return carry

        lax.fori_loop(0, SEQ // QB, body, 0, unroll=2)

    def finish(blk, carry):
        rows = pl.ds(pl.multiple_of(blk * QB, QB), QB)
        lses = [ref[rows, :] for ref in lse_refs]
        top = jnp.maximum(jnp.maximum(lses[0], lses[1]), lses[2])
        wts = [jnp.exp(lse - top) for lse in lses]
        num = wts[0] * o0_s[rows, :] + wts[1] * o1_s[rows, :] + wts[2] * o2_s[rows, :]
        den = wts[0] + wts[1] + wts[2]
        o_ref[rows, :] = (num / den * z_ref[rows, :]).astype(o_ref.dtype)
        return carry

    lax.fori_loop(0, SEQ // QB, finish, 0)


def _attn_prompt(proj, bias):
    n_pairs = W_B // LANES
    col = lambda first: (lambda b, p: (b, first * (D_MODEL // LANES) + p))
    return pl.pallas_call(
        _attn_prompt_kernel,
        grid=(BATCH, n_pairs),
        in_specs=[pl.BlockSpec((SEQ, LANES), col(3)),
                  pl.BlockSpec((SEQ, LANES), col(4)),
                  pl.BlockSpec((SEQ, LANES), col(5)),
                  pl.BlockSpec((SEQ, LANES), col(6)),
                  pl.BlockSpec((2, QB, 2 * QB), lambda b, p: (0, 0, 0))],
        out_specs=pl.BlockSpec((SEQ, LANES), lambda b, p: (b, p)),
        out_shape=jax.ShapeDtypeStruct((ROWS, W_B), BF16),
        scratch_shapes=[pltpu.VMEM((SEQ, LANES), BF16)] * 2
                       + [pltpu.VMEM((QB + SEQ, LANES), BF16)] * 2
                       + [pltpu.VMEM((SEQ, LANES), F32)] * 6,
        compiler_params=_params("arbitrary", "arbitrary"),
        name="attn_prompt",
    )(proj, proj, proj, proj, bias)


def _sample_key_counts():
    i = (np.arange(HEADS_PER_STEP * DEC_SEQ) % DEC_SEQ)[:, None]

    def count(dist):
        total = np.zeros(dist.shape, np.float32)
        for window, dil in DILATION_GROUPS:
            total += (dist >= 0) & (dist <= window) & (dist % dil == 0)
        return total

    cache = count(CACHE_LEN + i - np.arange(CACHE_LEN)[None, :])
    new = count(i - np.arange(QB)[None, :]) * (np.arange(QB)[None, :] < DEC_SEQ)
    return cache, new.astype(np.float32)


def _attn_sample_kernel(q_ref, kn_ref, vn_ref, z_ref, kt_ref, vt_ref, cc_ref, cn_ref, b_any, o_ref):
    del b_any
    width = HEADS_PER_STEP * HD
    n_q = HEADS_PER_STEP * DEC_SEQ
    lane = lax.broadcasted_iota(jnp.int32, (n_q, width), 1)
    row = lax.broadcasted_iota(jnp.int32, (n_q, width), 0)
    own_head = (lane // HD) == (row // DEC_SEQ)
    q_rep = jnp.concatenate([q_ref[...]] * HEADS_PER_STEP, axis=0)
    q_big = jnp.where(own_head, q_rep, 0.0).astype(BF16)
    pad = jnp.zeros((QB - DEC_SEQ, width), F32)
    kn = jnp.concatenate([kn_ref[...], pad], axis=0).astype(BF16)
    vn = jnp.concatenate([vn_ref[...], pad], axis=0).astype(BF16)
    cc = cc_ref[...]
    cn = cn_ref[...]
    kt = kt_ref[...].reshape(width, CACHE_LEN).astype(BF16)
    s_c = jnp.where(cc > 0.0, _dot(q_big, kt), NEG)
    s_n = jnp.where(cn > 0.0, _dot_nt(q_big, kn), NEG)
    m = jnp.maximum(jnp.max(s_c, axis=1, keepdims=True), jnp.max(s_n, axis=1, keepdims=True))
    e_c = jnp.exp(s_c - m) * cc
    e_n = jnp.exp(s_n - m) * cn
    l = jnp.sum(e_c, axis=1, keepdims=True) + jnp.sum(e_n, axis=1, keepdims=True)
    vt = vt_ref[...].reshape(width, CACHE_LEN).astype(BF16)
    out_big = (_dot_nt(e_c.astype(BF16), vt) + _dot(e_n.astype(BF16), vn)) / l
    out_big = jnp.where(own_head, out_big, 0.0)
    out = jnp.sum(out_big.reshape(HEADS_PER_STEP, DEC_SEQ, width), axis=0)
    o_ref[...] = (out * z_ref[...]).astype(o_ref.dtype)


def _attn_sample(proj, kt_all, vt_all, e, cc, cn, b_buf):
    first = ROWS_P // DEC_SEQ
    width = HEADS_PER_STEP * HD
    n_steps = H_B // HEADS_PER_STEP
    col = lambda c: (lambda b, s: (first + b, c * (D_MODEL // width) + s))
    cache_spec = pl.BlockSpec((None, None, HEADS_PER_STEP, HD, CACHE_LEN), lambda b, s: (e, b, s, 0, 0))
    n_q = HEADS_PER_STEP * DEC_SEQ
    return pl.pallas_call(
        _attn_sample_kernel,
        grid=(DEC_BATCH, n_steps),
        in_specs=[pl.BlockSpec((DEC_SEQ, width), col(3)),
                  pl.BlockSpec((DEC_SEQ, width), col(4)),
                  pl.BlockSpec((DEC_SEQ, width), col(5)),
                  pl.BlockSpec((DEC_SEQ, width), col(6)),
                  cache_spec, cache_spec,
                  pl.BlockSpec((n_q, CACHE_LEN), lambda b, s: (0, 0)),
                  pl.BlockSpec((n_q, QB), lambda b, s: (0, 0)),
                  pl.BlockSpec(memory_space=pl.ANY)],
        out_specs=pl.BlockSpec((DEC_SEQ, width), lambda b, s: (first + b, s)),
        out_shape=jax.ShapeDtypeStruct((ROWS, W_B), BF16),
        input_output_aliases={8: 0},
        compiler_params=_params("arbitrary", "arbitrary"),
        name="attn_sample",
    )(proj, proj, proj, proj, kt_all, vt_all, cc, cn, b_buf)


def _out_proj_kernel(a_ref, b_ref, w_ref, h_ref, g_ref, h_out_ref, n_out_ref):
    half = w_ref.shape[0] // 2
    h_new = h_ref[...] + _dot(a_ref[...], w_ref[0:half, :]) + _dot(b_ref[...], w_ref[half:, :])
    h_out_ref[...] = h_new
    ms = jnp.mean(h_new * h_new, axis=-1, keepdims=True)
    n_out_ref[...] = (h_new * lax.rsqrt(ms + EPS) * g_ref[...]).astype(n_out_ref.dtype)


def _out_proj(a, a_col, b, b_col, w, h, g, norm_dtype):
    half = w.shape[0] // 2
    return pl.pallas_call(
        _out_proj_kernel,
        grid=(N_TILES,),
        in_specs=[pl.BlockSpec((TM, half), lambda i: (i, a_col)),
                  pl.BlockSpec((TM, half), lambda i: (i, b_col)),
                  pl.BlockSpec((2 * half, D_MODEL), lambda i: (0, 0)),
                  pl.BlockSpec((TM, D_MODEL), lambda i: (i, 0)),
                  pl.BlockSpec((1, D_MODEL), lambda i: (0, 0))],
        out_specs=[pl.BlockSpec((TM, D_MODEL), lambda i: (i, 0)),
                   pl.BlockSpec((TM, D_MODEL), lambda i: (i, 0))],
        out_shape=[jax.ShapeDtypeStruct((ROWS, D_MODEL), F32),
                   jax.ShapeDtypeStruct((ROWS, D_MODEL), norm_dtype)],
        compiler_params=_params("arbitrary"),
        name="out_proj",
    )(a, b, w, h, g.reshape(1, D_MODEL))


TN_ODD = 512
N_COL_ODD = W_C // TN_ODD
PRE_SLOTS = BATCH + ROWS_S // TM


def _odd_in_kernel(x_ref, wb_ref, wc_ref, wx_ref, wz_ref, cw_ref, e1_ref, e2_ref, y_ref, pre_ref, buf):
    i = pl.program_id(1)

    @pl.when(i == 0)
    def _():
        buf[0:SUBLANES, :] = jnp.zeros((SUBLANES, TN_ODD), F32)

    x = x_ref[...]
    pre = _dot(x, wc_ref[...]) * _dot(x, wx_ref[...])
    pre_ref[...] = pre
    buf[SUBLANES:SUBLANES + TM, :] = pre
    row = lax.broadcasted_iota(jnp.int32, (TM, TN_ODD), 0)
    t = jnp.where(i < N_TILES_P, (i % TILES_PER_SEQ) * TM + row, row % DEC_SEQ)
    pre_m1 = jnp.where(t >= 1, buf[SUBLANES - 1:SUBLANES - 1 + TM, :], e1_ref[...])
    pre_m2 = jnp.where(t >= 2, buf[SUBLANES - 2:SUBLANES - 2 + TM, :], e2_ref[...])
    conv = cw_ref[0:1, :] * pre_m2 + cw_ref[1:2, :] * pre_m1 + cw_ref[2:3, :] * pre
    y = _dot(x, wb_ref[...]) * conv * _silu(_dot(x, wz_ref[...]))
    y_ref[...] = y.astype(y_ref.dtype)
    buf[0:SUBLANES, :] = pre[TM - SUBLANES:, :]


def _odd_in(n, w, conv_w, e1, e2):
    wcol = lambda part: (lambda c, i: (0, part * N_COL_ODD + c))
    e_block = lambda c, i: (jnp.where(i < N_TILES_P, 0, i - N_TILES_P + 1), c)
    pre_block = lambda c, i: (jnp.where(i < N_TILES_P, i // TILES_PER_SEQ, i - N_TILES_P + BATCH), c)
    return pl.pallas_call(
        _odd_in_kernel,
        grid=(N_COL_ODD, N_TILES),
        in_specs=[pl.BlockSpec((TM, D_MODEL), lambda c, i: (i, 0)),
                  pl.BlockSpec((D_MODEL, TN_ODD), wcol(0)),
                  pl.BlockSpec((D_MODEL, TN_ODD), wcol(1)),
                  pl.BlockSpec((D_MODEL, TN_ODD), wcol(2)),
                  pl.BlockSpec((D_MODEL, TN_ODD), wcol(3)),
                  pl.BlockSpec((CONV_W, TN_ODD), lambda c, i: (0, c)),
                  pl.BlockSpec((TM, TN_ODD), e_block),
                  pl.BlockSpec((TM, TN_ODD), e_block)],
        out_specs=[pl.BlockSpec((TM, TN_ODD), lambda c, i: (i, c)),
                   pl.BlockSpec((TM, TN_ODD), pre_block)],
        out_shape=[jax.ShapeDtypeStruct((ROWS, W_C), BF16),
                   jax.ShapeDtypeStruct((PRE_SLOTS * TM, W_C), F32)],
        scratch_shapes=[pltpu.VMEM((SUBLANES + TM, TN_ODD), F32)],
        compiler_params=_params("arbitrary", "arbitrary"),
        name="odd_in",
    )(n, w, w, w, w, conv_w, e1, e2)


def _rope_tables():
    half = HD // 2
    inv = 1.0 / (ROPE_THETA ** (jnp.arange(half, dtype=F32) / half))
    pos_s = PAST_LEN + (jnp.arange(TM) % DEC_SEQ)
    pos = jnp.concatenate([jnp.arange(SEQ), pos_s]).astype(F32)
    ang = pos[:, None] * inv[None, :]
    cos = jnp.cos(ang)
    sin = jnp.sin(ang)
    reps = LANES // HD
    return (jnp.tile(jnp.concatenate([cos, cos], axis=1), (1, reps)),
            jnp.tile(jnp.concatenate([-sin, sin], axis=1), (1, reps)))


def _gmlp_weights(w_s, b_s):
    tril = jnp.tril(jnp.ones((CHUNK, CHUNK), bool))
    w_p = jnp.where(tril[None], w_s, 0.0)
    small = w_p[:, :DEC_SEQ, :DEC_SEQ]
    eye = jnp.eye(CHUNK // DEC_SEQ, dtype=F32)
    w_smp = jnp.einsum('ab,hts->hatbs', eye, small).reshape(H_A, CHUNK, CHUNK)
    wm = jnp.stack([w_p, w_smp]).astype(BF16)
    bias_p = jnp.repeat(b_s.T, CH_A, axis=1)
    bias_s = jnp.tile(bias_p[:DEC_SEQ], (CHUNK // DEC_SEQ, 1))
    return wm, jnp.stack([bias_p, bias_s])


def _conv_edges(state):
    e1 = jnp.zeros((DEC_BATCH, DEC_SEQ, W_C), F32).at[:, 0].set(state[:, 1])
    e2 = jnp.zeros((DEC_BATCH, DEC_SEQ, W_C), F32).at[:, 0].set(state[:, 0]).at[:, 1].set(state[:, 1])
    pad = jnp.zeros((TM, W_C), F32)
    return (jnp.concatenate([pad, e1.reshape(ROWS_S, W_C)]), jnp.concatenate([pad, e2.reshape(ROWS_S, W_C)]))


def kernel(x_prompt, x_sample, cache_b_k, cache_b_v, state_c_conv, norm_w, final_norm_w,
           w_in_even, w_s, b_s, w_out_even, w_in_odd, conv_w, w_out_odd):
    h = jnp.concatenate([x_prompt.reshape(ROWS_P, D_MODEL), x_sample.reshape(ROWS_S, D_MODEL)])
    cos, sin = _rope_tables()
    band_bias = jnp.asarray(_band_bias())
    cc, cn = (jnp.asarray(c) for c in _sample_key_counts())
    kt_all = jnp.transpose(cache_b_k, (0, 1, 3, 4, 2))
    vt_all = jnp.transpose(cache_b_v, (0, 1, 3, 4, 2))
    n = _rmsnorm(h, norm_w[0])
    kp_l, vp_l, ks_l, vs_l, av_l, cp_l, cs_l = [], [], [], [], [], [], []
    buf_p = min(W_MAX, SEQ)
    for layer in range(DEPTH):
        last = layer == DEPTH - 1
        g_next = final_norm_w if last else norm_w[layer + 1]
        norm_dtype = F32 if last else BF16
        if layer % 2 == 0:
            e = layer // 2
            proj = _even_in(n, w_in_even[e].astype(BF16), cos, sin)
            wm, bias = _gmlp_weights(w_s[e], b_s[e])
            a_out = _gmlp(proj, wm, bias)
            b_out = _attn_prompt(proj, band_bias)
            b_out = _attn_sample(proj, kt_all, vt_all, e, cc, cn, b_out)
            h, n = _out_proj(a_out, 0, b_out, 0, w_out_even[e].astype(BF16), h, g_next, norm_dtype)
            for first_col, dst_p, dst_s in ((4 * D_MODEL, kp_l, ks_l), (5 * D_MODEL, vp_l, vs_l)):
                for b in range(BATCH):
                    row0 = b * SEQ + SEQ - buf_p
                    dst_p.append(lax.slice(proj, (row0, first_col), (row0 + buf_p, first_col + W_B)))
                dst_s.append(lax.slice(proj, (ROWS_P, first_col), (ROWS, first_col + W_B)))
            av_l.append(lax.slice(proj, (ROWS_P, W_A), (ROWS, 2 * W_A)))
        else:
            c = layer // 2
            e1, e2 = _conv_edges(state_c_conv[c])
            y, pre = _odd_in(n, w_in_odd[c].astype(BF16), conv_w[c], e1, e2)
            h, n = _out_proj(y, 0, y, 1, w_out_odd[c].astype(BF16), h, g_next, norm_dtype)
            pre_p = pre[:BATCH * TM].reshape(BATCH, TM, W_C)
            cp_l.append(pre_p[:, TM - (CONV_W - 1):])
            cs_l.append(pre[BATCH * TM:].reshape(DEC_BATCH, DEC_SEQ, W_C)[:, DEC_SEQ - (CONV_W - 1):])
    y_prompt = n[:ROWS_P].reshape(BATCH, SEQ, D_MODEL)
    y_sample = n[ROWS_P:].reshape(DEC_BATCH, DEC_SEQ, D_MODEL)
    new_b_k_prompt = jnp.stack(kp_l).reshape(N_EVEN, BATCH, buf_p, H_B, HD)
    new_b_v_prompt = jnp.stack(vp_l).reshape(N_EVEN, BATCH, buf_p, H_B, HD)
    new_b_k_sample = jnp.stack(ks_l).reshape(N_EVEN, DEC_BATCH, DEC_SEQ, H_B, HD)
    new_b_v_sample = jnp.stack(vs_l).reshape(N_EVEN, DEC_BATCH, DEC_SEQ, H_B, HD)
    new_a_v_sample = jnp.stack(av_l).reshape(N_EVEN, DEC_BATCH, DEC_SEQ, W_A)
    return (y_prompt, y_sample, new_b_k_prompt, new_b_v_prompt, new_b_k_sample, new_b_v_sample,
            new_a_v_sample, jnp.stack(cp_l), jnp.stack(cs_l))
```

```python
import numpy as np
import jax
import jax.numpy as jnp
from jax import lax
from jax.experimental import pallas as pl
from jax.experimental.pallas import tpu as pltpu

D_MODEL = 1024
BATCH = 4
SEQ = 4096
DEPTH = 4
DEC_BATCH = 128
DEC_SEQ = 8
PAST_LEN = 2048
N_EVEN = (DEPTH + 1) // 2
N_ODD = DEPTH // 2
W_A = D_MODEL
CHUNK = 128
H_A = 8
CH_A = W_A // H_A
W_B = D_MODEL
H_B = 16
HD = W_B // H_B
DILATION_GROUPS = ((128, 1), (512, 4), (2048, 16))
W_MAX = 2048
ROPE_THETA = 10000.0
W_C = 2 * D_MODEL
CONV_W = 3
EPS = 1e-6
NEG = -1e30
EVEN_IN = 3 * W_A + 4 * W_B
ODD_IN = 4 * W_C

LANES = 128
SUBLANES = 8
ROWS_P = BATCH * SEQ
ROWS_S = DEC_BATCH * DEC_SEQ
ROWS = ROWS_P + ROWS_S
TM = 512
N_TILES = ROWS // TM
N_TILES_P = ROWS_P // TM
TILES_PER_SEQ = SEQ // TM
QB = 128
CACHE_LEN = min(W_MAX, PAST_LEN)
HEADS_PER_STEP = 8
VMEM_LIMIT = 48 * 1024 * 1024

assert all(window // dil == QB for window, dil in DILATION_GROUPS)

F32 = jnp.float32
BF16 = jnp.bfloat16


def _dot(a, b):
    return jnp.dot(a, b, preferred_element_type=F32)


def _dot_nt(a, b):
    return lax.dot_general(a, b, (((1,), (1,)), ((), ())), preferred_element_type=F32)


def _params(*sem):
    return pltpu.CompilerParams(dimension_semantics=sem, vmem_limit_bytes=VMEM_LIMIT)


def _rmsnorm_kernel(x_ref, g_ref, o_ref):
    x = x_ref[...]
    ms = jnp.mean(x * x, axis=-1, keepdims=True)
    o_ref[...] = (x * lax.rsqrt(ms + EPS) * g_ref[...]).astype(o_ref.dtype)


def _rmsnorm(x, g):
    return pl.pallas_call(
        _rmsnorm_kernel,
        grid=(N_TILES,),
        in_specs=[pl.BlockSpec((TM, D_MODEL), lambda i: (i, 0)),
                  pl.BlockSpec((1, D_MODEL), lambda i: (0, 0))],
        out_specs=pl.BlockSpec((TM, D_MODEL), lambda i: (i, 0)),
        out_shape=jax.ShapeDtypeStruct((ROWS, D_MODEL), BF16),
        compiler_params=_params("arbitrary"),
        name="rmsnorm",
    )(x, g.reshape(1, D_MODEL))


def _gelu(x):
    return 0.5 * x * (1.0 + lax.erf(x * np.float32(np.sqrt(0.5))))


def _silu(x):
    return x * jax.nn.sigmoid(x)


TM_E = 256
N_TILES_E = ROWS // TM_E
N_TILES_E_P = ROWS_P // TM_E
ATTN_COLS = 4 * W_B


def _rope(x, cos, sin, o_ref, first_col, scale):
    lane = lax.broadcasted_iota(jnp.int32, (TM_E, LANES), 1)
    first_half = (lane % HD) < (HD // 2)
    for c in range(W_B // LANES):
        xc = x[:, c * LANES:(c + 1) * LANES]
        partner = jnp.where(first_half, pltpu.roll(xc, LANES - HD // 2, 1), pltpu.roll(xc, HD // 2, 1))
        out = xc * cos + partner * sin
        if scale is not None:
            out = out * scale
        o_ref[:, first_col + c * LANES:first_col + (c + 1) * LANES] = out


def _even_in_kernel(x_ref, w_ref, cos_ref, sin_ref, wm_ref, bias_ref, a_ref, av_ref, o_ref):
    x = x_ref[...]
    col = lambda j: w_ref[:, j * D_MODEL:(j + 1) * D_MODEL]
    v = _gelu(_dot(x, col(1)))
    av_ref[...] = v
    v_bf = v.astype(BF16)
    u = _gelu(_dot(x, col(0)))
    z = _silu(_dot(x, col(2)))
    for c in range(TM_E // CHUNK):
        rows = slice(c * CHUNK, (c + 1) * CHUNK)
        for h in range(H_A):
            sl = slice(h * CH_A, (h + 1) * CH_A)
            mix = _dot(wm_ref[h], v_bf[rows, sl]) + bias_ref[:, sl]
            a_ref[rows, sl] = (u[rows, sl] * mix * z[rows, sl]).astype(a_ref.dtype)
    cos = cos_ref[...]
    sin = sin_ref[...]
    _rope(_dot(x, col(3)), cos, sin, o_ref, 0, np.float32(HD ** -0.5))
    _rope(_dot(x, col(4)), cos, sin, o_ref, W_B, None)
    o_ref[:, 2 * W_B:3 * W_B] = _dot(x, col(5))
    o_ref[:, 3 * W_B:4 * W_B] = _silu(_dot(x, col(6)))


def _even_in(n, w, cos, sin, wm, bias):
    table = lambda i: (jnp.where(i < N_TILES_E_P, i % (SEQ // TM_E), SEQ // TM_E), 0)
    variant = lambda i: jnp.where(i < N_TILES_E_P, 0, 1)
    return pl.pallas_call(
        _even_in_kernel,
        grid=(N_TILES_E,),
        in_specs=[pl.BlockSpec((TM_E, D_MODEL), lambda i: (i, 0)),
                  pl.BlockSpec((D_MODEL, EVEN_IN), lambda i: (0, 0), pipeline_mode=pl.Buffered(1)),
                  pl.BlockSpec((TM_E, LANES), table),
                  pl.BlockSpec((TM_E, LANES), table),
                  pl.BlockSpec((None, H_A, CHUNK, CHUNK), lambda i: (variant(i), 0, 0, 0)),
                  pl.BlockSpec((None, CHUNK, W_A), lambda i: (variant(i), 0, 0))],
        out_specs=[pl.BlockSpec((TM_E, W_A), lambda i: (i, 0)),
                   pl.BlockSpec((TM_E, W_A), lambda i: (jnp.maximum(i - N_TILES_E_P, 0), 0)),
                   pl.BlockSpec((TM_E, ATTN_COLS), lambda i: (i, 0))],
        out_shape=[jax.ShapeDtypeStruct((ROWS, W_A), BF16),
                   jax.ShapeDtypeStruct((ROWS_S, W_A), F32),
                   jax.ShapeDtypeStruct((ROWS, ATTN_COLS), F32)],
        compiler_params=_params("arbitrary"),
        name="even_in",
    )(n, w, cos, sin, wm, bias)


def _band_bias():
    i = np.arange(QB)[:, None]
    j = np.arange(2 * QB)[None, :]
    band = (j - i >= 0) & (j - i <= QB)
    first = band & (j >= QB)
    return np.where(np.stack([band, first]), 0.0, NEG).astype(np.float32)


def _attn_prompt_kernel(q_ref, k_ref, v_ref, z_ref, bias_ref, o_ref,
                        q0_s, q1_s, k_s, v_s, o0_s, o1_s, o2_s, lse0_s, lse1_s, lse2_s):
    o_refs = (o0_s, o1_s, o2_s)
    lse_refs = (lse0_s, lse1_s, lse2_s)
    head0 = lax.broadcasted_iota(jnp.int32, (QB, LANES), 1) < HD
    ones = jnp.ones((2 * QB, LANES), BF16)
    k_s[0:QB, :] = jnp.zeros((QB, LANES), BF16)
    v_s[0:QB, :] = jnp.zeros((QB, LANES), BF16)

    for g, (_, dil) in enumerate(DILATION_GROUPS):
        n = SEQ // dil
        ch = min(n, 4 * QB)
        n_ch = n // ch
        n_blk = n // QB

        def deinterleave(idx, carry, dil=dil, ch=ch, n_ch=n_ch):
            start = idx // n_ch + dil * ch * (idx % n_ch)
            src = pl.ds(pl.multiple_of(start, ch), ch) if dil == 1 else pl.ds(start, ch, stride=dil)
            dst = pl.ds(pl.multiple_of(idx * ch, QB), ch)
            dst_pad = pl.ds(pl.multiple_of(QB + idx * ch, QB), ch)
            first = lax.broadcasted_iota(jnp.int32, (ch, LANES), 1) < HD
            q = q_ref[src, :]
            q0_s[dst, :] = jnp.where(first, q, 0.0).astype(BF16)
            q1_s[dst, :] = jnp.where(first, 0.0, q).astype(BF16)
            k_s[dst_pad, :] = k_ref[src, :].astype(BF16)
            v_s[dst_pad, :] = v_ref[src, :].astype(BF16)
            return carry

        lax.fori_loop(0, SEQ // ch, deinterleave, 0)

        def body(blk, carry, g=g, dil=dil, n_blk=n_blk):
            res = blk // n_blk
            pos = blk % n_blk
            rows = pl.ds(pl.multiple_of(blk * QB, QB), QB)
            rows2 = pl.ds(pl.multiple_of(blk * QB, QB), 2 * QB)
            k2 = k_s[rows2, :]
            v_aug = jnp.concatenate([v_s[rows2, :], ones], axis=1)
            bias = bias_ref[jnp.where(pos > 0, 0, 1)]
            q2 = jnp.concatenate([q0_s[rows, :], q1_s[rows, :]], axis=0)
            s = _dot_nt(q2, k2) + jnp.concatenate([bias, bias], axis=0)
            m = jnp.max(s, axis=1, keepdims=True)
            r = _dot(jnp.exp(s - m).astype(BF16), v_aug)
            r = jnp.where(jnp.concatenate([head0, head0], axis=1), r[:QB], r[QB:])
            l = r[:, LANES:]
            start = res + dil * QB * pos
            dst = pl.ds(pl.multiple_of(start, QB), QB) if dil == 1 else pl.ds(start, QB, stride=dil)
            o_refs[g][dst, :] = r[:, :LANES] / l
            lse_refs[g][dst, :] = jnp.where(head0, m[:QB], m[QB:]) + jnp.log(l)
            return carry

        lax.fori_loop(0, SEQ // QB, body, 0, unroll=8)

    def finish(blk, carry):
        rows = pl.ds(pl.multiple_of(blk * QB, QB), QB)
        lses = [ref[rows, :] for ref in lse_refs]
        top = jnp.maximum(jnp.maximum(lses[0], lses[1]), lses[2])
        wts = [jnp.exp(lse - top) for lse in lses]
        num = wts[0] * o0_s[rows, :] + wts[1] * o1_s[rows, :] + wts[2] * o2_s[rows, :]
        den = wts[0] + wts[1] + wts[2]
        o_ref[rows, :] = (num / den * z_ref[rows, :]).astype(o_ref.dtype)
        return carry

    lax.fori_loop(0, SEQ // QB, finish, 0)


def _attn_prompt(proj, bias):
    n_pairs = W_B // LANES
    col = lambda first: (lambda b, p: (b, first * (W_B // LANES) + p))
    return pl.pallas_call(
        _attn_prompt_kernel,
        grid=(BATCH, n_pairs),
        in_specs=[pl.BlockSpec((SEQ, LANES), col(0)),
                  pl.BlockSpec((SEQ, LANES), col(1)),
                  pl.BlockSpec((SEQ, LANES), col(2)),
                  pl.BlockSpec((SEQ, LANES), col(3)),
                  pl.BlockSpec((2, QB, 2 * QB), lambda b, p: (0, 0, 0))],
        out_specs=pl.BlockSpec((SEQ, LANES), lambda b, p: (b, p)),
        out_shape=jax.ShapeDtypeStruct((ROWS, W_B), BF16),
        scratch_shapes=[pltpu.VMEM((SEQ, LANES), BF16)] * 2
                       + [pltpu.VMEM((QB + SEQ, LANES), BF16)] * 2
                       + [pltpu.VMEM((SEQ, LANES), F32)] * 6,
        compiler_params=_params("arbitrary", "arbitrary"),
        name="attn_prompt",
    )(proj, proj, proj, proj, bias)


def _sample_key_counts():
    i = (np.arange(HEADS_PER_STEP * DEC_SEQ) % DEC_SEQ)[:, None]

    def count(dist):
        total = np.zeros(dist.shape, np.float32)
        for window, dil in DILATION_GROUPS:
            total += (dist >= 0) & (dist <= window) & (dist % dil == 0)
        return total

    cache = count(CACHE_LEN + i - np.arange(CACHE_LEN)[None, :])
    new = count(i - np.arange(QB)[None, :]) * (np.arange(QB)[None, :] < DEC_SEQ)
    return cache, new.astype(np.float32)


def _attn_sample_kernel(q_ref, kn_ref, vn_ref, z_ref, kt_ref, vt_ref, cc_ref, cn_ref, b_any, o_ref):
    del b_any
    width = HEADS_PER_STEP * HD
    n_q = HEADS_PER_STEP * DEC_SEQ
    lane = lax.broadcasted_iota(jnp.int32, (n_q, width), 1)
    row = lax.broadcasted_iota(jnp.int32, (n_q, width), 0)
    own_head = (lane // HD) == (row // DEC_SEQ)
    q_rep = jnp.concatenate([q_ref[...]] * HEADS_PER_STEP, axis=0)
    q_big = jnp.where(own_head, q_rep, 0.0).astype(BF16)
    pad = jnp.zeros((QB - DEC_SEQ, width), F32)
    kn = jnp.concatenate([kn_ref[...], pad], axis=0).astype(BF16)
    vn = jnp.concatenate([vn_ref[...], pad], axis=0).astype(BF16)
    cc = cc_ref[...]
    cn = cn_ref[...]
    kt = kt_ref[...].reshape(width, CACHE_LEN).astype(BF16)
    s_c = jnp.where(cc > 0.0, _dot(q_big, kt), NEG)
    s_n = jnp.where(cn > 0.0, _dot_nt(q_big, kn), NEG)
    m = jnp.maximum(jnp.max(s_c, axis=1, keepdims=True), jnp.max(s_n, axis=1, keepdims=True))
    e_c = jnp.exp(s_c - m) * cc
    e_n = jnp.exp(s_n - m) * cn
    l = jnp.sum(e_c, axis=1, keepdims=True) + jnp.sum(e_n, axis=1, keepdims=True)
    vt = vt_ref[...].reshape(width, CACHE_LEN).astype(BF16)
    out_big = (_dot_nt(e_c.astype(BF16), vt) + _dot(e_n.astype(BF16), vn)) / l
    out_big = jnp.where(own_head, out_big, 0.0)
    out = jnp.sum(out_big.reshape(HEADS_PER_STEP, DEC_SEQ, width), axis=0)
    o_ref[...] = (out * z_ref[...]).astype(o_ref.dtype)


def _attn_sample(proj, kt_all, vt_all, e, cc, cn, b_buf):
    first = ROWS_P // DEC_SEQ
    width = HEADS_PER_STEP * HD
    n_steps = H_B // HEADS_PER_STEP
    col = lambda c: (lambda b, s: (first + b, c * (W_B // width) + s))
    cache_spec = pl.BlockSpec((None, None, HEADS_PER_STEP, HD, CACHE_LEN), lambda b, s: (e, b, s, 0, 0))
    n_q = HEADS_PER_STEP * DEC_SEQ
    return pl.pallas_call(
        _attn_sample_kernel,
        grid=(DEC_BATCH, n_steps),
        in_specs=[pl.BlockSpec((DEC_SEQ, width), col(0)),
                  pl.BlockSpec((DEC_SEQ, width), col(1)),
                  pl.BlockSpec((DEC_SEQ, width), col(2)),
                  pl.BlockSpec((DEC_SEQ, width), col(3)),
                  cache_spec, cache_spec,
                  pl.BlockSpec((n_q, CACHE_LEN), lambda b, s: (0, 0)),
                  pl.BlockSpec((n_q, QB), lambda b, s: (0, 0)),
                  pl.BlockSpec(memory_space=pl.ANY)],
        out_specs=pl.BlockSpec((DEC_SEQ, width), lambda b, s: (first + b, s)),
        out_shape=jax.ShapeDtypeStruct((ROWS, W_B), BF16),
        input_output_aliases={8: 0},
        compiler_params=_params("arbitrary", "arbitrary"),
        name="attn_sample",
    )(proj, proj, proj, proj, kt_all, vt_all, cc, cn, b_buf)


def _out_proj_kernel(a_ref, b_ref, w_ref, h_ref, g_ref, h_out_ref, n_out_ref):
    half = w_ref.shape[0] // 2
    h_new = h_ref[...] + _dot(a_ref[...], w_ref[0:half, :]) + _dot(b_ref[...], w_ref[half:, :])
    h_out_ref[...] = h_new
    ms = jnp.mean(h_new * h_new, axis=-1, keepdims=True)
    n_out_ref[...] = (h_new * lax.rsqrt(ms + EPS) * g_ref[...]).astype(n_out_ref.dtype)


def _out_proj(a, a_col, b, b_col, w, h, g, norm_dtype):
    half = w.shape[0] // 2
    return pl.pallas_call(
        _out_proj_kernel,
        grid=(N_TILES,),
        in_specs=[pl.BlockSpec((TM, half), lambda i: (i, a_col)),
                  pl.BlockSpec((TM, half), lambda i: (i, b_col)),
                  pl.BlockSpec((2 * half, D_MODEL), lambda i: (0, 0)),
                  pl.BlockSpec((TM, D_MODEL), lambda i: (i, 0)),
                  pl.BlockSpec((1, D_MODEL), lambda i: (0, 0))],
        out_specs=[pl.BlockSpec((TM, D_MODEL), lambda i: (i, 0)),
                   pl.BlockSpec((TM, D_MODEL), lambda i: (i, 0))],
        out_shape=[jax.ShapeDtypeStruct((ROWS, D_MODEL), F32),
                   jax.ShapeDtypeStruct((ROWS, D_MODEL), norm_dtype)],
        compiler_params=_params("arbitrary"),
        name="out_proj",
    )(a, b, w, h, g.reshape(1, D_MODEL))


TN_ODD = 512
N_COL_ODD = W_C // TN_ODD
PRE_SLOTS = BATCH + ROWS_S // TM


def _odd_in_kernel(x_ref, wb_ref, wc_ref, wx_ref, wz_ref, cw_ref, e1_ref, e2_ref, y_ref, pre_ref, buf):
    i = pl.program_id(1)

    @pl.when(i == 0)
    def _():
        buf[0:SUBLANES, :] = jnp.zeros((SUBLANES, TN_ODD), F32)

    x = x_ref[...]
    pre = _dot(x, wc_ref[...]) * _dot(x, wx_ref[...])
    pre_ref[...] = pre
    buf[SUBLANES:SUBLANES + TM, :] = pre
    row = lax.broadcasted_iota(jnp.int32, (TM, TN_ODD), 0)
    t = jnp.where(i < N_TILES_P, (i % TILES_PER_SEQ) * TM + row, row % DEC_SEQ)
    pre_m1 = jnp.where(t >= 1, buf[SUBLANES - 1:SUBLANES - 1 + TM, :], e1_ref[...])
    pre_m2 = jnp.where(t >= 2, buf[SUBLANES - 2:SUBLANES - 2 + TM, :], e2_ref[...])
    conv = cw_ref[0:1, :] * pre_m2 + cw_ref[1:2, :] * pre_m1 + cw_ref[2:3, :] * pre
    y = _dot(x, wb_ref[...]) * conv * _silu(_dot(x, wz_ref[...]))
    y_ref[...] = y.astype(y_ref.dtype)
    buf[0:SUBLANES, :] = pre[TM - SUBLANES:, :]


def _odd_in(n, w, conv_w, e1, e2):
    wcol = lambda part: (lambda c, i: (0, part * N_COL_ODD + c))
    e_block = lambda c, i: (jnp.where(i < N_TILES_P, 0, i - N_TILES_P + 1), c)
    pre_block = lambda c, i: (jnp.where(i < N_TILES_P, i // TILES_PER_SEQ, i - N_TILES_P + BATCH), c)
    return pl.pallas_call(
        _odd_in_kernel,
        grid=(N_COL_ODD, N_TILES),
        in_specs=[pl.BlockSpec((TM, D_MODEL), lambda c, i: (i, 0)),
                  pl.BlockSpec((D_MODEL, TN_ODD), wcol(0)),
                  pl.BlockSpec((D_MODEL, TN_ODD), wcol(1)),
                  pl.BlockSpec((D_MODEL, TN_ODD), wcol(2)),
                  pl.BlockSpec((D_MODEL, TN_ODD), wcol(3)),
                  pl.BlockSpec((CONV_W, TN_ODD), lambda c, i: (0, c)),
                  pl.BlockSpec((TM, TN_ODD), e_block),
                  pl.BlockSpec((TM, TN_ODD), e_block)],
        out_specs=[pl.BlockSpec((TM, TN_ODD), lambda c, i: (i, c)),
                   pl.BlockSpec((TM, TN_ODD), pre_block)],
        out_shape=[jax.ShapeDtypeStruct((ROWS, W_C), BF16),
                   jax.ShapeDtypeStruct((PRE_SLOTS * TM, W_C), F32)],
        scratch_shapes=[pltpu.VMEM((SUBLANES + TM, TN_ODD), F32)],
        compiler_params=_params("arbitrary", "arbitrary"),
        name="odd_in",
    )(n, w, w, w, w, conv_w, e1, e2)


def _rope_tables():
    half = HD // 2
    inv = 1.0 / (ROPE_THETA ** (jnp.arange(half, dtype=F32) / half))
    pos_s = PAST_LEN + (jnp.arange(TM_E) % DEC_SEQ)
    pos = jnp.concatenate([jnp.arange(SEQ), pos_s]).astype(F32)
    ang = pos[:, None] * inv[None, :]
    cos = jnp.cos(ang)
    sin = jnp.sin(ang)
    reps = LANES // HD
    return (jnp.tile(jnp.concatenate([cos, cos], axis=1), (1, reps)),
            jnp.tile(jnp.concatenate([-sin, sin], axis=1), (1, reps)))


def _gmlp_weights(w_s, b_s):
    tril = jnp.tril(jnp.ones((CHUNK, CHUNK), bool))
    w_p = jnp.where(tril[None], w_s, 0.0)
    small = w_p[:, :DEC_SEQ, :DEC_SEQ]
    eye = jnp.eye(CHUNK // DEC_SEQ, dtype=F32)
    w_smp = jnp.einsum('ab,hts->hatbs', eye, small).reshape(H_A, CHUNK, CHUNK)
    wm = jnp.stack([w_p, w_smp]).astype(BF16)
    bias_p = jnp.repeat(b_s.T, CH_A, axis=1)
    bias_s = jnp.tile(bias_p[:DEC_SEQ], (CHUNK // DEC_SEQ, 1))
    return wm, jnp.stack([bias_p, bias_s])


def _conv_edges(state):
    e1 = jnp.zeros((DEC_BATCH, DEC_SEQ, W_C), F32).at[:, 0].set(state[:, 1])
    e2 = jnp.zeros((DEC_BATCH, DEC_SEQ, W_C), F32).at[:, 0].set(state[:, 0]).at[:, 1].set(state[:, 1])
    pad = jnp.zeros((TM, W_C), F32)
    return (jnp.concatenate([pad, e1.reshape(ROWS_S, W_C)]), jnp.concatenate([pad, e2.reshape(ROWS_S, W_C)]))


def kernel(x_prompt, x_sample, cache_b_k, cache_b_v, state_c_conv, norm_w, final_norm_w,
           w_in_even, w_s, b_s, w_out_even, w_in_odd, conv_w, w_out_odd):
    h = jnp.concatenate([x_prompt.reshape(ROWS_P, D_MODEL), x_sample.reshape(ROWS_S, D_MODEL)])
    cos, sin = _rope_tables()
    band_bias = jnp.asarray(_band_bias())
    cc, cn = (jnp.asarray(c) for c in _sample_key_counts())
    kt_all = jnp.transpose(cache_b_k, (0, 1, 3, 4, 2))
    vt_all = jnp.transpose(cache_b_v, (0, 1, 3, 4, 2))
    n = _rmsnorm(h, norm_w[0])
    kp_l, vp_l, ks_l, vs_l, av_l, cp_l, cs_l = [], [], [], [], [], [], []
    buf_p = min(W_MAX, SEQ)
    for layer in range(DEPTH):
        last = layer == DEPTH - 1
        g_next = final_norm_w if last else norm_w[layer + 1]
        norm_dtype = F32 if last else BF16
        if layer % 2 == 0:
            e = layer // 2
            wm, bias = _gmlp_weights(w_s[e], b_s[e])
            a_out, av, proj = _even_in(n, w_in_even[e].astype(BF16), cos, sin, wm, bias)
            b_out = _attn_prompt(proj, band_bias)
            b_out = _attn_sample(proj, kt_all, vt_all, e, cc, cn, b_out)
            h, n = _out_proj(a_out, 0, b_out, 0, w_out_even[e].astype(BF16), h, g_next, norm_dtype)
            for first_col, dst_p, dst_s in ((W_B, kp_l, ks_l), (2 * W_B, vp_l, vs_l)):
                for b in range(BATCH):
                    row0 = b * SEQ + SEQ - buf_p
                    dst_p.append(lax.slice(proj, (row0, first_col), (row0 + buf_p, first_col + W_B)))
                dst_s.append(lax.slice(proj, (ROWS_P, first_col), (ROWS, first_col + W_B)))
            av_l.append(av)
        else:
            c = layer // 2
            e1, e2 = _conv_edges(state_c_conv[c])
            y, pre = _odd_in(n, w_in_odd[c].astype(BF16), conv_w[c], e1, e2)
            h, n = _out_proj(y, 0, y, 1, w_out_odd[c].astype(BF16), h, g_next, norm_dtype)
            pre_p = pre[:BATCH * TM].reshape(BATCH, TM, W_C)
            cp_l.append(pre_p[:, TM - (CONV_W - 1):])
            cs_l.append(pre[BATCH * TM:].reshape(DEC_BATCH, DEC_SEQ, W_C)[:, DEC_SEQ - (CONV_W - 1):])
    y_prompt = n[:ROWS_P].reshape(BATCH, SEQ, D_MODEL)
    y_sample = n[ROWS_P:].reshape(DEC_BATCH, DEC_SEQ, D_MODEL)
    new_b_k_prompt = jnp.stack(kp_l).reshape(N_EVEN, BATCH, buf_p, H_B, HD)
    new_b_v_prompt = jnp.stack(vp_l).reshape(N_EVEN, BATCH, buf_p, H_B, HD)
    new_b_k_sample = jnp.stack(ks_l).reshape(N_EVEN, DEC_BATCH, DEC_SEQ, H_B, HD)
    new_b_v_sample = jnp.stack(vs_l).reshape(N_EVEN, DEC_BATCH, DEC_SEQ, H_B, HD)
    new_a_v_sample = jnp.stack(av_l).reshape(N_EVEN, DEC_BATCH, DEC_SEQ, W_A)
    return (y_prompt, y_sample, new_b_k_prompt, new_b_v_prompt, new_b_k_sample, new_b_v_sample,
            new_a_v_sample, jnp.stack(cp_l), jnp.stack(cs_l))
```

```python
import functools

import numpy as np
import jax
import jax.numpy as jnp
from jax import lax
from jax.experimental import pallas as pl
from jax.experimental.pallas import tpu as pltpu

D_MODEL = 1024
BATCH = 4
SEQ = 4096
DEPTH = 4
DEC_BATCH = 128
DEC_SEQ = 8
PAST_LEN = 2048
N_EVEN = (DEPTH + 1) // 2
N_ODD = DEPTH // 2
W_A = D_MODEL
CHUNK = 128
H_A = 8
CH_A = W_A // H_A
W_B = D_MODEL
H_B = 16
HD = W_B // H_B
DILATION_GROUPS = ((128, 1), (512, 4), (2048, 16))
W_MAX = 2048
ROPE_THETA = 10000.0
W_C = 2 * D_MODEL
CONV_W = 3
EPS = 1e-6
NEG = -1e30
EVEN_IN = 3 * W_A + 4 * W_B
ODD_IN = 4 * W_C

LANES = 128
SUBLANES = 8
ROWS_P = BATCH * SEQ
ROWS_S = DEC_BATCH * DEC_SEQ
ROWS = ROWS_P + ROWS_S
TM = 512
N_TILES = ROWS // TM
N_TILES_P = ROWS_P // TM
TILES_PER_SEQ = SEQ // TM
QB = 128
CACHE_LEN = min(W_MAX, PAST_LEN)
HEADS_PER_STEP = 8
VMEM_LIMIT = 48 * 1024 * 1024
ATTN_VMEM_LIMIT = 56 * 1024 * 1024

assert all(window // dil == QB for window, dil in DILATION_GROUPS)

F32 = jnp.float32
BF16 = jnp.bfloat16


def _dot(a, b):
    return jnp.dot(a, b, preferred_element_type=F32)


def _dot_nt(a, b):
    return lax.dot_general(a, b, (((1,), (1,)), ((), ())), preferred_element_type=F32)


def _params(*sem):
    return pltpu.CompilerParams(dimension_semantics=sem, vmem_limit_bytes=VMEM_LIMIT)


def _rmsnorm_kernel(*refs):
    x_refs, (g_ref, o_ref) = refs[:-2], refs[-2:]
    x = _read_tile(x_refs)
    ms = jnp.mean(x * x, axis=-1, keepdims=True)
    o_ref[...] = (x * lax.rsqrt(ms + EPS) * g_ref[...]).astype(o_ref.dtype)


def _rmsnorm(x, g):
    x_specs, x_args = _row_tiles(x, D_MODEL)
    return pl.pallas_call(
        _rmsnorm_kernel,
        grid=(N_TILES,),
        in_specs=x_specs + [pl.BlockSpec((1, D_MODEL), lambda i: (0, 0))],
        out_specs=pl.BlockSpec((TM, D_MODEL), lambda i: (i, 0)),
        out_shape=jax.ShapeDtypeStruct((ROWS, D_MODEL), BF16),
        compiler_params=_params("arbitrary"),
        name="rmsnorm",
    )(*x_args, g.reshape(1, D_MODEL))


def _gelu(x):
    return 0.5 * x * (1.0 + lax.erf(x * np.float32(np.sqrt(0.5))))


def _silu(x):
    return x * jax.nn.sigmoid(x)


TM_E = 256
N_TILES_E = ROWS // TM_E
N_TILES_E_P = ROWS_P // TM_E
ATTN_COLS = 4 * W_B


def _rope(x, cos, sin, o_ref, first_col, scale):
    lane = lax.broadcasted_iota(jnp.int32, (TM_E, LANES), 1)
    first_half = (lane % HD) < (HD // 2)
    for c in range(W_B // LANES):
        xc = x[:, c * LANES:(c + 1) * LANES]
        partner = jnp.where(first_half, pltpu.roll(xc, LANES - HD // 2, 1), pltpu.roll(xc, HD // 2, 1))
        out = xc * cos + partner * sin
        if scale is not None:
            out = out * scale
        o_ref[:, first_col + c * LANES:first_col + (c + 1) * LANES] = out


def _even_in_kernel(x_ref, w_ref, cos_ref, sin_ref, wm_ref, bias_ref, a_ref, av_ref, o_ref):
    x = x_ref[...]
    col = lambda j: w_ref[:, j * D_MODEL:(j + 1) * D_MODEL]
    v = _gelu(_dot(x, col(1)))
    av_ref[...] = v
    v_bf = v.astype(BF16)
    u = _gelu(_dot(x, col(0)))
    z = _silu(_dot(x, col(2)))
    for c in range(TM_E // CHUNK):
        rows = slice(c * CHUNK, (c + 1) * CHUNK)
        for h in range(H_A):
            sl = slice(h * CH_A, (h + 1) * CH_A)
            mix = _dot(wm_ref[h], v_bf[rows, sl]) + bias_ref[:, sl]
            a_ref[rows, sl] = (u[rows, sl] * mix * z[rows, sl]).astype(a_ref.dtype)
    cos = cos_ref[...]
    sin = sin_ref[...]
    _rope(_dot(x, col(3)), cos, sin, o_ref, 0, np.float32(HD ** -0.5))
    _rope(_dot(x, col(4)), cos, sin, o_ref, W_B, None)
    o_ref[:, 2 * W_B:3 * W_B] = _dot(x, col(5))
    o_ref[:, 3 * W_B:4 * W_B] = _silu(_dot(x, col(6)))


def _even_in(n, w, cos, sin, wm, bias):
    table = lambda i: (jnp.where(i < N_TILES_E_P, i % (SEQ // TM_E), SEQ // TM_E), 0)
    variant = lambda i: jnp.where(i < N_TILES_E_P, 0, 1)
    return pl.pallas_call(
        _even_in_kernel,
        grid=(N_TILES_E,),
        in_specs=[pl.BlockSpec((TM_E, D_MODEL), lambda i: (i, 0)),
                  pl.BlockSpec((D_MODEL, EVEN_IN), lambda i: (0, 0), pipeline_mode=pl.Buffered(1)),
                  pl.BlockSpec((TM_E, LANES), table),
                  pl.BlockSpec((TM_E, LANES), table),
                  pl.BlockSpec((None, H_A, CHUNK, CHUNK), lambda i: (variant(i), 0, 0, 0)),
                  pl.BlockSpec((None, CHUNK, W_A), lambda i: (variant(i), 0, 0))],
        out_specs=[pl.BlockSpec((TM_E, W_A), lambda i: (i, 0)),
                   pl.BlockSpec((TM_E, W_A), lambda i: (jnp.maximum(i - N_TILES_E_P, 0), 0)),
                   pl.BlockSpec((TM_E, ATTN_COLS), lambda i: (i, 0))],
        out_shape=[jax.ShapeDtypeStruct((ROWS, W_A), BF16),
                   jax.ShapeDtypeStruct((ROWS_S, W_A), F32),
                   jax.ShapeDtypeStruct((ROWS, ATTN_COLS), F32)],
        compiler_params=_params("arbitrary"),
        name="even_in",
    )(n, w, cos, sin, wm, bias)


def _band_bias():
    i = np.arange(QB)[:, None]
    j = np.arange(2 * QB)[None, :]
    band = (j - i >= 0) & (j - i <= QB)
    first = band & (j >= QB)
    return np.where(np.stack([band, first]), 0.0, NEG).astype(np.float32)


STEPS_PER_UNIT = 8
HALF_BLOCKS = SEQ // QB // 2


def _prompt_attention_step(phase, q_ref, k_ref, v_ref, z_ref, bias_ref, o_ref,
                           q0_s, q1_s, k_s, v_s, o0_s, o1_s, o2_s, lse0_s, lse1_s, lse2_s):
    o_refs = (o0_s, o1_s, o2_s)
    lse_refs = (lse0_s, lse1_s, lse2_s)
    head0 = lax.broadcasted_iota(jnp.int32, (QB, LANES), 1) < HD
    ones = jnp.ones((2 * QB, LANES), BF16)

    @pl.when(phase == 0)
    def _():
        k_s[0:QB, :] = jnp.zeros((QB, LANES), BF16)
        v_s[0:QB, :] = jnp.zeros((QB, LANES), BF16)

    for g, (_, dil) in enumerate(DILATION_GROUPS):
        n = SEQ // dil
        ch = min(n, 4 * QB)
        n_ch = n // ch
        n_blk = n // QB

        def deinterleave(idx, carry, dil=dil, ch=ch, n_ch=n_ch):
            start = idx // n_ch + dil * ch * (idx % n_ch)
            src = pl.ds(pl.multiple_of(start, ch), ch) if dil == 1 else pl.ds(start, ch, stride=dil)
            dst = pl.ds(pl.multiple_of(idx * ch, QB), ch)
            dst_pad = pl.ds(pl.multiple_of(QB + idx * ch, QB), ch)
            first = lax.broadcasted_iota(jnp.int32, (ch, LANES), 1) < HD
            q = q_ref[src, :]
            q0_s[dst, :] = jnp.where(first, q, 0.0).astype(BF16)
            q1_s[dst, :] = jnp.where(first, 0.0, q).astype(BF16)
            k_s[dst_pad, :] = k_ref[src, :].astype(BF16)
            v_s[dst_pad, :] = v_ref[src, :].astype(BF16)
            return carry

        @pl.when(phase == 2 * g)
        def _(ch=ch, deinterleave=deinterleave):
            lax.fori_loop(0, SEQ // ch, deinterleave, 0)

        def body(i, carry, g=g, dil=dil, n_blk=n_blk):
            blk = (phase % 2) * HALF_BLOCKS + i
            res = blk // n_blk
            pos = blk % n_blk
            rows = pl.ds(pl.multiple_of(blk * QB, QB), QB)
            rows2 = pl.ds(pl.multiple_of(blk * QB, QB), 2 * QB)
            k2 = k_s[rows2, :]
            v_aug = jnp.concatenate([v_s[rows2, :], ones], axis=1)
            bias = bias_ref[jnp.where(pos > 0, 0, 1)]
            q2 = jnp.concatenate([q0_s[rows, :], q1_s[rows, :]], axis=0)
            s = _dot_nt(q2, k2) + jnp.concatenate([bias, bias], axis=0)
            m = jnp.max(s, axis=1, keepdims=True)
            r = _dot(jnp.exp(s - m).astype(BF16), v_aug)
            r = jnp.where(jnp.concatenate([head0, head0], axis=1), r[:QB], r[QB:])
            l = r[:, LANES:]
            start = res + dil * QB * pos
            dst = pl.ds(pl.multiple_of(start, QB), QB) if dil == 1 else pl.ds(start, QB, stride=dil)
            o_refs[g][dst, :] = r[:, :LANES] / l
            lse_refs[g][dst, :] = jnp.where(head0, m[:QB], m[QB:]) + jnp.log(l)
            return carry

        @pl.when(phase // 2 == g)
        def _(body=body):
            lax.fori_loop(0, HALF_BLOCKS, body, 0, unroll=8)

    def finish(i, carry):
        blk = (phase % 2) * HALF_BLOCKS + i
        rows = pl.ds(pl.multiple_of(blk * QB, QB), QB)
        lses = [ref[rows, :] for ref in lse_refs]
        top = jnp.maximum(jnp.maximum(lses[0], lses[1]), lses[2])
        wts = [jnp.exp(lse - top) for lse in lses]
        num = wts[0] * o0_s[rows, :] + wts[1] * o1_s[rows, :] + wts[2] * o2_s[rows, :]
        den = wts[0] + wts[1] + wts[2]
        o_ref[rows, :] = (num / den * z_ref[rows, :]).astype(o_ref.dtype)
        return carry

    @pl.when(phase // 2 == len(DILATION_GROUPS))
    def _():
        lax.fori_loop(0, HALF_BLOCKS, finish, 0)


def _sample_key_counts():
    i = (np.arange(HEADS_PER_STEP * DEC_SEQ) % DEC_SEQ)[:, None]

    def count(dist):
        total = np.zeros(dist.shape, np.float32)
        for window, dil in DILATION_GROUPS:
            total += (dist >= 0) & (dist <= window) & (dist % dil == 0)
        return total

    cache = count(CACHE_LEN + i - np.arange(CACHE_LEN)[None, :])
    new = count(i - np.arange(QB)[None, :]) * (np.arange(QB)[None, :] < DEC_SEQ)
    return cache, new.astype(np.float32)


def _sample_attention_step(q_ref, kn_ref, vn_ref, z_ref, kt_ref, vt_ref, cc_ref, cn_ref, o_ref):
    width = HEADS_PER_STEP * HD
    n_q = HEADS_PER_STEP * DEC_SEQ
    lane = lax.broadcasted_iota(jnp.int32, (n_q, width), 1)
    row = lax.broadcasted_iota(jnp.int32, (n_q, width), 0)
    own_head = (lane // HD) == (row // DEC_SEQ)
    q_rep = jnp.concatenate([q_ref[...]] * HEADS_PER_STEP, axis=0)
    q_big = jnp.where(own_head, q_rep, 0.0).astype(BF16)
    pad = jnp.zeros((QB - DEC_SEQ, width), F32)
    kn = jnp.concatenate([kn_ref[...], pad], axis=0).astype(BF16)
    vn = jnp.concatenate([vn_ref[...], pad], axis=0).astype(BF16)
    cc = cc_ref[...]
    cn = cn_ref[...]
    kt = kt_ref[...].reshape(width, CACHE_LEN).astype(BF16)
    s_c = jnp.where(cc > 0.0, _dot(q_big, kt), NEG)
    s_n = jnp.where(cn > 0.0, _dot_nt(q_big, kn), NEG)
    m = jnp.maximum(jnp.max(s_c, axis=1, keepdims=True), jnp.max(s_n, axis=1, keepdims=True))
    e_c = jnp.exp(s_c - m) * cc
    e_n = jnp.exp(s_n - m) * cn
    l = jnp.sum(e_c, axis=1, keepdims=True) + jnp.sum(e_n, axis=1, keepdims=True)
    vt = vt_ref[...].reshape(width, CACHE_LEN).astype(BF16)
    out_big = (_dot_nt(e_c.astype(BF16), vt) + _dot(e_n.astype(BF16), vn)) / l
    out_big = jnp.where(own_head, out_big, 0.0)
    out = jnp.sum(out_big.reshape(HEADS_PER_STEP, DEC_SEQ, width), axis=0)
    o_ref[...] = (out * z_ref[...]).astype(o_ref.dtype)


def _attn_kernel(qs_ref, kn_ref, vn_ref, zs_ref, kt_ref, vt_ref, cc_ref, cn_ref,
                 qp_ref, kp_ref, vp_ref, zp_ref, bias_ref, os_ref, op_ref, *scratch):
    _sample_attention_step(qs_ref, kn_ref, vn_ref, zs_ref, kt_ref, vt_ref, cc_ref, cn_ref, os_ref)
    _prompt_attention_step(pl.program_id(0) % STEPS_PER_UNIT, qp_ref, kp_ref, vp_ref, zp_ref, bias_ref, op_ref,
                           *scratch)


def _attn(proj, kt_all, vt_all, e, cc, cn, bias):
    first = ROWS_P // DEC_SEQ
    width = HEADS_PER_STEP * HD
    per_row = H_B // HEADS_PER_STEP
    n_pairs = W_B // LANES
    n_steps = DEC_BATCH * per_row
    assert n_steps == BATCH * n_pairs * STEPS_PER_UNIT
    n_q = HEADS_PER_STEP * DEC_SEQ
    once = pl.Buffered(1)
    s_col = lambda c: (lambda s: (first + s // per_row, c * per_row + s % per_row))
    p_col = lambda c: (lambda s: (s // STEPS_PER_UNIT // n_pairs, c * n_pairs + s // STEPS_PER_UNIT % n_pairs))
    cache_spec = pl.BlockSpec((None, None, HEADS_PER_STEP, HD, CACHE_LEN),
                              lambda s: (e, s // per_row, s % per_row, 0, 0))
    return pl.pallas_call(
        _attn_kernel,
        grid=(n_steps,),
        in_specs=[pl.BlockSpec((DEC_SEQ, width), s_col(0)),
                  pl.BlockSpec((DEC_SEQ, width), s_col(1)),
                  pl.BlockSpec((DEC_SEQ, width), s_col(2)),
                  pl.BlockSpec((DEC_SEQ, width), s_col(3)),
                  cache_spec, cache_spec,
                  pl.BlockSpec((n_q, CACHE_LEN), lambda s: (0, 0), pipeline_mode=once),
                  pl.BlockSpec((n_q, QB), lambda s: (0, 0), pipeline_mode=once),
                  pl.BlockSpec((SEQ, LANES), p_col(0), pipeline_mode=once),
                  pl.BlockSpec((SEQ, LANES), p_col(1), pipeline_mode=once),
                  pl.BlockSpec((SEQ, LANES), p_col(2), pipeline_mode=once),
                  pl.BlockSpec((SEQ, LANES), p_col(3), pipeline_mode=once),
                  pl.BlockSpec((2, QB, 2 * QB), lambda s: (0, 0, 0), pipeline_mode=once)],
        out_specs=[pl.BlockSpec((DEC_SEQ, width), lambda s: (s // per_row, s % per_row)),
                   pl.BlockSpec((SEQ, LANES), p_col(0))],
        out_shape=[jax.ShapeDtypeStruct((ROWS_S, W_B), BF16),
                   jax.ShapeDtypeStruct((ROWS_P, W_B), BF16)],
        scratch_shapes=[pltpu.VMEM((SEQ, LANES), BF16)] * 2
                       + [pltpu.VMEM((QB + SEQ, LANES), BF16)] * 2
                       + [pltpu.VMEM((SEQ, LANES), F32)] * 6,
        compiler_params=pltpu.CompilerParams(dimension_semantics=("arbitrary",),
                                             vmem_limit_bytes=ATTN_VMEM_LIMIT),
        name="attn",
    )(proj, proj, proj, proj, kt_all, vt_all, cc, cn, proj, proj, proj, proj, bias)


def _row_tiles(x, width, col=0):
    if not isinstance(x, tuple):
        return [pl.BlockSpec((TM, width), lambda i: (i, col))], [x]
    x_p, x_s = x
    return ([pl.BlockSpec((TM, width), lambda i: (jnp.minimum(i, N_TILES_P - 1), col)),
             pl.BlockSpec((TM, width), lambda i: (jnp.maximum(i - N_TILES_P, 0), col))],
            [x_p, x_s])


def _read_tile(refs):
    if len(refs) == 1:
        return refs[0][...]
    return jnp.where(pl.program_id(0) < N_TILES_P, refs[0][...], refs[1][...])


def _out_proj_kernel(n_b, n_h, a_ref, *refs):
    b_refs, refs = refs[:n_b], refs[n_b:]
    w_ref, refs = refs[0], refs[1:]
    h_refs, (g_ref, h_out_ref, n_out_ref) = refs[:n_h], refs[n_h:]
    half = w_ref.shape[0] // 2
    h_new = (_read_tile(h_refs) + _dot(a_ref[...], w_ref[0:half, :])
             + _dot(_read_tile(b_refs), w_ref[half:, :]))
    h_out_ref[...] = h_new
    ms = jnp.mean(h_new * h_new, axis=-1, keepdims=True)
    n_out_ref[...] = (h_new * lax.rsqrt(ms + EPS) * g_ref[...]).astype(n_out_ref.dtype)


def _out_proj(a, a_col, b, b_col, w, h, g, norm_dtype):
    half = w.shape[0] // 2
    b_specs, b_args = _row_tiles(b, half, b_col)
    h_specs, h_args = _row_tiles(h, D_MODEL)
    return pl.pallas_call(
        functools.partial(_out_proj_kernel, len(b_args), len(h_args)),
        grid=(N_TILES,),
        in_specs=[pl.BlockSpec((TM, half), lambda i: (i, a_col))] + b_specs
                 + [pl.BlockSpec((2 * half, D_MODEL), lambda i: (0, 0))] + h_specs
                 + [pl.BlockSpec((1, D_MODEL), lambda i: (0, 0))],
        out_specs=[pl.BlockSpec((TM, D_MODEL), lambda i: (i, 0)),
                   pl.BlockSpec((TM, D_MODEL), lambda i: (i, 0))],
        out_shape=[jax.ShapeDtypeStruct((ROWS, D_MODEL), F32),
                   jax.ShapeDtypeStruct((ROWS, D_MODEL), norm_dtype)],
        compiler_params=_params("arbitrary"),
        name="out_proj",
    )(a, *b_args, w, *h_args, g.reshape(1, D_MODEL))


TN_ODD = 512
N_COL_ODD = W_C // TN_ODD
PRE_SLOTS = BATCH + ROWS_S // TM


def _odd_in_kernel(x_ref, wb_ref, wc_ref, wx_ref, wz_ref, cw_ref, e1_ref, e2_ref, y_ref, pre_ref, buf):
    i = pl.program_id(1)

    @pl.when(i == 0)
    def _():
        buf[0:SUBLANES, :] = jnp.zeros((SUBLANES, TN_ODD), F32)

    x = x_ref[...]
    pre = _dot(x, wc_ref[...]) * _dot(x, wx_ref[...])
    pre_ref[...] = pre
    buf[SUBLANES:SUBLANES + TM, :] = pre
    row = lax.broadcasted_iota(jnp.int32, (TM, TN_ODD), 0)
    t = jnp.where(i < N_TILES_P, (i % TILES_PER_SEQ) * TM + row, row % DEC_SEQ)
    is_sample = i >= N_TILES_P
    pre_m1 = jnp.where(t >= 1, buf[SUBLANES - 1:SUBLANES - 1 + TM, :], jnp.where(is_sample, e1_ref[...], 0.0))
    pre_m2 = jnp.where(t >= 2, buf[SUBLANES - 2:SUBLANES - 2 + TM, :], jnp.where(is_sample, e2_ref[...], 0.0))
    conv = cw_ref[0:1, :] * pre_m2 + cw_ref[1:2, :] * pre_m1 + cw_ref[2:3, :] * pre
    y = _dot(x, wb_ref[...]) * conv * _silu(_dot(x, wz_ref[...]))
    y_ref[...] = y.astype(y_ref.dtype)
    buf[0:SUBLANES, :] = pre[TM - SUBLANES:, :]


def _odd_in(n, w, conv_w, e1, e2):
    wcol = lambda part: (lambda c, i: (0, part * N_COL_ODD + c))
    e_block = lambda c, i: (jnp.maximum(i - N_TILES_P, 0), c)
    pre_block = lambda c, i: (jnp.where(i < N_TILES_P, i // TILES_PER_SEQ, i - N_TILES_P + BATCH), c)
    return pl.pallas_call(
        _odd_in_kernel,
        grid=(N_COL_ODD, N_TILES),
        in_specs=[pl.BlockSpec((TM, D_MODEL), lambda c, i: (i, 0)),
                  pl.BlockSpec((D_MODEL, TN_ODD), wcol(0)),
                  pl.BlockSpec((D_MODEL, TN_ODD), wcol(1)),
                  pl.BlockSpec((D_MODEL, TN_ODD), wcol(2)),
                  pl.BlockSpec((D_MODEL, TN_ODD), wcol(3)),
                  pl.BlockSpec((CONV_W, TN_ODD), lambda c, i: (0, c)),
                  pl.BlockSpec((TM, TN_ODD), e_block),
                  pl.BlockSpec((TM, TN_ODD), e_block)],
        out_specs=[pl.BlockSpec((TM, TN_ODD), lambda c, i: (i, c)),
                   pl.BlockSpec((TM, TN_ODD), pre_block)],
        out_shape=[jax.ShapeDtypeStruct((ROWS, W_C), BF16),
                   jax.ShapeDtypeStruct((PRE_SLOTS * TM, W_C), F32)],
        scratch_shapes=[pltpu.VMEM((SUBLANES + TM, TN_ODD), F32)],
        compiler_params=_params("arbitrary", "arbitrary"),
        name="odd_in",
    )(n, w, w, w, w, conv_w, e1, e2)


def _rope_tables():
    half = HD // 2
    inv = 1.0 / (ROPE_THETA ** (jnp.arange(half, dtype=F32) / half))
    pos_s = PAST_LEN + (jnp.arange(TM_E) % DEC_SEQ)
    pos = jnp.concatenate([jnp.arange(SEQ), pos_s]).astype(F32)
    ang = pos[:, None] * inv[None, :]
    cos = jnp.cos(ang)
    sin = jnp.sin(ang)
    reps = LANES // HD
    return (jnp.tile(jnp.concatenate([cos, cos], axis=1), (1, reps)),
            jnp.tile(jnp.concatenate([-sin, sin], axis=1), (1, reps)))


def _gmlp_weights(w_s, b_s):
    tril = jnp.tril(jnp.ones((CHUNK, CHUNK), bool))
    w_p = jnp.where(tril[None], w_s, 0.0)
    small = w_p[:, :DEC_SEQ, :DEC_SEQ]
    eye = jnp.eye(CHUNK // DEC_SEQ, dtype=F32)
    w_smp = jnp.einsum('ab,hts->hatbs', eye, small).reshape(H_A, CHUNK, CHUNK)
    wm = jnp.stack([w_p, w_smp]).astype(BF16)
    bias_p = jnp.repeat(b_s.T, CH_A, axis=1)
    bias_s = jnp.tile(bias_p[:DEC_SEQ], (CHUNK // DEC_SEQ, 1))
    return wm, jnp.stack([bias_p, bias_s])


def _conv_edges(state):
    e1 = jnp.pad(state[:, CONV_W - 2:], ((0, 0), (0, DEC_SEQ - 1), (0, 0)))
    e2 = jnp.pad(state, ((0, 0), (0, DEC_SEQ - (CONV_W - 1)), (0, 0)))
    return e1.reshape(ROWS_S, W_C), e2.reshape(ROWS_S, W_C)


def kernel(x_prompt, x_sample, cache_b_k, cache_b_v, state_c_conv, norm_w, final_norm_w,
           w_in_even, w_s, b_s, w_out_even, w_in_odd, conv_w, w_out_odd):
    h = (x_prompt.reshape(ROWS_P, D_MODEL), x_sample.reshape(ROWS_S, D_MODEL))
    cos, sin = _rope_tables()
    band_bias = jnp.asarray(_band_bias())
    cc, cn = (jnp.asarray(c) for c in _sample_key_counts())
    kt_all = jnp.transpose(cache_b_k, (0, 1, 3, 4, 2))
    vt_all = jnp.transpose(cache_b_v, (0, 1, 3, 4, 2))
    n = _rmsnorm(h, norm_w[0])
    kp_l, vp_l, ks_l, vs_l, av_l, cp_l, cs_l = [], [], [], [], [], [], []
    buf_p = min(W_MAX, SEQ)
    for layer in range(DEPTH):
        last = layer == DEPTH - 1
        g_next = final_norm_w if last else norm_w[layer + 1]
        norm_dtype = F32 if last else BF16
        if layer % 2 == 0:
            e = layer // 2
            wm, bias = _gmlp_weights(w_s[e], b_s[e])
            a_out, av, proj = _even_in(n, w_in_even[e].astype(BF16), cos, sin, wm, bias)
            att_s, att_p = _attn(proj, kt_all, vt_all, e, cc, cn, band_bias)
            h, n = _out_proj(a_out, 0, (att_p, att_s), 0, w_out_even[e].astype(BF16), h, g_next, norm_dtype)
            for first_col, dst_p, dst_s in ((W_B, kp_l, ks_l), (2 * W_B, vp_l, vs_l)):
                for b in range(BATCH):
                    row0 = b * SEQ + SEQ - buf_p
                    dst_p.append(lax.slice(proj, (row0, first_col), (row0 + buf_p, first_col + W_B)))
                dst_s.append(lax.slice(proj, (ROWS_P, first_col), (ROWS, first_col + W_B)))
            av_l.append(av)
        else:
            c = layer // 2
            e1, e2 = _conv_edges(state_c_conv[c])
            y, pre = _odd_in(n, w_in_odd[c].astype(BF16), conv_w[c], e1, e2)
            h, n = _out_proj(y, 0, y, 1, w_out_odd[c].astype(BF16), h, g_next, norm_dtype)
            pre_p = pre[:BATCH * TM].reshape(BATCH, TM, W_C)
            cp_l.append(pre_p[:, TM - (CONV_W - 1):])
            cs_l.append(pre[BATCH * TM:].reshape(DEC_BATCH, DEC_SEQ, W_C)[:, DEC_SEQ - (CONV_W - 1):])
    y_prompt = n[:ROWS_P].reshape(BATCH, SEQ, D_MODEL)
    y_sample = n[ROWS_P:].reshape(DEC_BATCH, DEC_SEQ, D_MODEL)
    new_b_k_prompt = jnp.stack(kp_l).reshape(N_EVEN, BATCH, buf_p, H_B, HD)
    new_b_v_prompt = jnp.stack(vp_l).reshape(N_EVEN, BATCH, buf_p, H_B, HD)
    new_b_k_sample = jnp.stack(ks_l).reshape(N_EVEN, DEC_BATCH, DEC_SEQ, H_B, HD)
    new_b_v_sample = jnp.stack(vs_l).reshape(N_EVEN, DEC_BATCH, DEC_SEQ, H_B, HD)
    new_a_v_sample = jnp.stack(av_l).reshape(N_EVEN, DEC_BATCH, DEC_SEQ, W_A)
    return (y_prompt, y_sample, new_b_k_prompt, new_b_v_prompt, new_b_k_sample, new_b_v_sample,
            new_a_v_sample, jnp.stack(cp_l), jnp.stack(cs_l))
```

```python
import functools

import numpy as np
import jax
import jax.numpy as jnp
from jax import lax
from jax.experimental import pallas as pl
from jax.experimental.pallas import tpu as pltpu

D_MODEL = 1024
BATCH = 4
SEQ = 4096
DEPTH = 4
DEC_BATCH = 128
DEC_SEQ = 8
PAST_LEN = 2048
N_EVEN = (DEPTH + 1) // 2
N_ODD = DEPTH // 2
W_A = D_MODEL
CHUNK = 128
H_A = 8
CH_A = W_A // H_A
W_B = D_MODEL
H_B = 16
HD = W_B // H_B
DILATION_GROUPS = ((128, 1), (512, 4), (2048, 16))
W_MAX = 2048
ROPE_THETA = 10000.0
W_C = 2 * D_MODEL
CONV_W = 3
EPS = 1e-6
NEG = -1e30
EVEN_IN = 3 * W_A + 4 * W_B
ODD_IN = 4 * W_C

LANES = 128
SUBLANES = 8
ROWS_P = BATCH * SEQ
ROWS_S = DEC_BATCH * DEC_SEQ
ROWS = ROWS_P + ROWS_S
TM = 512
N_TILES = ROWS // TM
N_TILES_P = ROWS_P // TM
TILES_PER_SEQ = SEQ // TM
QB = 128
CACHE_LEN = min(W_MAX, PAST_LEN)
HEADS_PER_STEP = 8
VMEM_LIMIT = 48 * 1024 * 1024
ATTN_VMEM_LIMIT = 56 * 1024 * 1024

assert all(window // dil == QB for window, dil in DILATION_GROUPS)

F32 = jnp.float32
BF16 = jnp.bfloat16


def _dot(a, b):
    return jnp.dot(a, b, preferred_element_type=F32)


def _dot_nt(a, b):
    return lax.dot_general(a, b, (((1,), (1,)), ((), ())), preferred_element_type=F32)


def _params(*sem):
    return pltpu.CompilerParams(dimension_semantics=sem, vmem_limit_bytes=VMEM_LIMIT)


def _rmsnorm_kernel(*refs):
    x_refs, (g_ref, o_ref) = refs[:-2], refs[-2:]
    x = _read_tile(x_refs)
    ms = jnp.mean(x * x, axis=-1, keepdims=True)
    o_ref[...] = (x * lax.rsqrt(ms + EPS) * g_ref[...]).astype(o_ref.dtype)


def _rmsnorm(x, g):
    x_specs, x_args = _row_tiles(x, D_MODEL)
    return pl.pallas_call(
        _rmsnorm_kernel,
        grid=(N_TILES,),
        in_specs=x_specs + [pl.BlockSpec((1, D_MODEL), lambda i: (0, 0))],
        out_specs=pl.BlockSpec((TM, D_MODEL), lambda i: (i, 0)),
        out_shape=jax.ShapeDtypeStruct((ROWS, D_MODEL), BF16),
        compiler_params=_params("arbitrary"),
        name="rmsnorm",
    )(*x_args, g.reshape(1, D_MODEL))


def _gelu(x):
    return 0.5 * x * (1.0 + lax.erf(x * np.float32(np.sqrt(0.5))))


def _silu(x):
    return x * jax.nn.sigmoid(x)


TM_E = 256
N_TILES_E = ROWS // TM_E
N_TILES_E_P = ROWS_P // TM_E
ATTN_COLS = 4 * W_B


def _rope(x, cos, sin, emit, first_chunk, scale):
    lane = lax.broadcasted_iota(jnp.int32, (TM_E, LANES), 1)
    first_half = (lane % HD) < (HD // 2)
    for c in range(W_B // LANES):
        xc = x[:, c * LANES:(c + 1) * LANES]
        partner = jnp.where(first_half, pltpu.roll(xc, LANES - HD // 2, 1), pltpu.roll(xc, HD // 2, 1))
        out = xc * cos + partner * sin
        if scale is not None:
            out = out * scale
        emit(first_chunk + c, out)


def _even_in_kernel(x_ref, w_ref, cos_ref, sin_ref, wm_ref, bias_ref, a_ref, av_ref, o_ref, *rest):
    perm_refs, stage = rest[:-1], rest[-1]

    def emit(chunk, val):
        cols = slice(chunk * LANES, (chunk + 1) * LANES)
        o_ref[:, cols] = val
        stage[chunk] = val
        for ref, (_, dil) in zip(perm_refs, DILATION_GROUPS):
            if dil == 1:
                ref[0, :, cols] = val.astype(BF16)
                continue
            for r in range(dil):
                ref[r, :, cols] = stage[chunk, pl.ds(r, TM_E // dil, stride=dil), :].astype(BF16)

    x = x_ref[...]
    col = lambda j: w_ref[:, j * D_MODEL:(j + 1) * D_MODEL]
    v = _gelu(_dot(x, col(1)))
    av_ref[...] = v
    v_bf = v.astype(BF16)
    u = _gelu(_dot(x, col(0)))
    z = _silu(_dot(x, col(2)))
    for c in range(TM_E // CHUNK):
        rows = slice(c * CHUNK, (c + 1) * CHUNK)
        for h in range(H_A):
            sl = slice(h * CH_A, (h + 1) * CH_A)
            mix = _dot(wm_ref[h], v_bf[rows, sl]) + bias_ref[:, sl]
            a_ref[rows, sl] = (u[rows, sl] * mix * z[rows, sl]).astype(a_ref.dtype)
    cos = cos_ref[...]
    sin = sin_ref[...]
    chunks = W_B // LANES
    _rope(_dot(x, col(3)), cos, sin, emit, 0, np.float32(HD ** -0.5))
    _rope(_dot(x, col(4)), cos, sin, emit, chunks, None)
    v_b = _dot(x, col(5))
    for c in range(chunks):
        emit(2 * chunks + c, v_b[:, c * LANES:(c + 1) * LANES])
    o_ref[:, 3 * W_B:4 * W_B] = _silu(_dot(x, col(6)))


def _even_in(n, w, cos, sin, wm, bias):
    tiles_per_seq = SEQ // TM_E
    table = lambda i: (jnp.where(i < N_TILES_E_P, i % tiles_per_seq, tiles_per_seq), 0)
    variant = lambda i: jnp.where(i < N_TILES_E_P, 0, 1)
    perm_block = lambda i: (jnp.where(i < N_TILES_E_P, i // tiles_per_seq, BATCH), 0,
                            jnp.where(i < N_TILES_E_P, i % tiles_per_seq, i - N_TILES_E_P), 0)
    perm_specs = [pl.BlockSpec((None, dil, TM_E // dil, 3 * W_B), perm_block) for _, dil in DILATION_GROUPS]
    perm_shapes = [jax.ShapeDtypeStruct((BATCH + 1, dil, SEQ // dil, 3 * W_B), BF16) for _, dil in DILATION_GROUPS]
    return pl.pallas_call(
        _even_in_kernel,
        grid=(N_TILES_E,),
        in_specs=[pl.BlockSpec((TM_E, D_MODEL), lambda i: (i, 0)),
                  pl.BlockSpec((D_MODEL, EVEN_IN), lambda i: (0, 0), pipeline_mode=pl.Buffered(1)),
                  pl.BlockSpec((TM_E, LANES), table),
                  pl.BlockSpec((TM_E, LANES), table),
                  pl.BlockSpec((None, H_A, CHUNK, CHUNK), lambda i: (variant(i), 0, 0, 0)),
                  pl.BlockSpec((None, CHUNK, W_A), lambda i: (variant(i), 0, 0))],
        out_specs=[pl.BlockSpec((TM_E, W_A), lambda i: (i, 0)),
                   pl.BlockSpec((TM_E, W_A), lambda i: (jnp.maximum(i - N_TILES_E_P, 0), 0)),
                   pl.BlockSpec((TM_E, ATTN_COLS), lambda i: (i, 0))] + perm_specs,
        out_shape=[jax.ShapeDtypeStruct((ROWS, W_A), BF16),
                   jax.ShapeDtypeStruct((ROWS_S, W_A), F32),
                   jax.ShapeDtypeStruct((ROWS, ATTN_COLS), F32)] + perm_shapes,
        scratch_shapes=[pltpu.VMEM((3 * W_B // LANES, TM_E, LANES), F32)],
        compiler_params=_params("arbitrary"),
        name="even_in",
    )(n, w, cos, sin, wm, bias)


def _band_bias():
    i = np.arange(QB)[:, None]
    j = np.arange(2 * QB)[None, :]
    band = (j - i >= 0) & (j - i <= QB)
    first = band & (j >= QB)
    own_first = j <= i
    return np.where(np.stack([band, first, own_first]), 0.0, NEG).astype(np.float32)


STEPS_PER_UNIT = 8
HALF_BLOCKS = SEQ // QB // 2


def _prompt_attention_step(phase, qkv_refs, z_ref, bias_ref, o_ref, o_refs, lse_refs):
    lane = lax.broadcasted_iota(jnp.int32, (QB, LANES), 1)
    head0 = lane < HD
    keep0 = head0.astype(BF16)
    keep1 = (lane >= HD).astype(BF16)
    ones = jnp.ones((2 * QB, LANES), BF16)

    for g, (_, dil) in enumerate(DILATION_GROUPS):
        q_ref, k_ref, v_ref = qkv_refs[3 * g:3 * g + 3]
        n_blk = SEQ // dil // QB

        def body(i, carry, g=g, dil=dil, n_blk=n_blk, q_ref=q_ref, k_ref=k_ref, v_ref=v_ref):
            blk = (phase % 2) * HALF_BLOCKS + i
            res = blk // n_blk
            pos = blk % n_blk
            rows = pl.ds(pl.multiple_of(blk * QB, QB), QB)
            rows2 = pl.ds(pl.multiple_of(jnp.maximum(blk - 1, 0) * QB, QB), 2 * QB)
            k2 = k_ref[rows2, :]
            v_aug = jnp.concatenate([v_ref[rows2, :], ones], axis=1)
            bias = bias_ref[jnp.where(blk == 0, 2, jnp.where(pos == 0, 1, 0))]
            q = q_ref[rows, :]
            q2 = jnp.concatenate([q * keep0, q * keep1], axis=0)
            s = _dot_nt(q2, k2) + jnp.concatenate([bias, bias], axis=0)
            m = jnp.max(s, axis=1, keepdims=True)
            r = _dot(jnp.exp(s - m).astype(BF16), v_aug)
            r = jnp.where(jnp.concatenate([head0, head0], axis=1), r[:QB], r[QB:])
            l = r[:, LANES:]
            start = res + dil * QB * pos
            dst = pl.ds(pl.multiple_of(start, QB), QB) if dil == 1 else pl.ds(start, QB, stride=dil)
            o_refs[g][dst, :] = r[:, :LANES] / l
            lse_refs[g][dst, :] = jnp.where(head0, m[:QB], m[QB:]) + jnp.log(l)
            return carry

        @pl.when(phase // 2 == g)
        def _(body=body):
            lax.fori_loop(0, HALF_BLOCKS, body, 0, unroll=8)

    def finish(i, carry):
        blk = (phase % 2) * HALF_BLOCKS + i
        rows = pl.ds(pl.multiple_of(blk * QB, QB), QB)
        lses = [ref[rows, :] for ref in lse_refs]
        top = jnp.maximum(jnp.maximum(lses[0], lses[1]), lses[2])
        wts = [jnp.exp(lse - top) for lse in lses]
        num = wts[0] * o_refs[0][rows, :] + wts[1] * o_refs[1][rows, :] + wts[2] * o_refs[2][rows, :]
        den = wts[0] + wts[1] + wts[2]
        o_ref[rows, :] = (num / den * z_ref[rows, :]).astype(o_ref.dtype)
        return carry

    @pl.when(phase // 2 == len(DILATION_GROUPS))
    def _():
        lax.fori_loop(0, HALF_BLOCKS, finish, 0)


def _sample_key_counts():
    i = (np.arange(HEADS_PER_STEP * DEC_SEQ) % DEC_SEQ)[:, None]

    def count(dist):
        total = np.zeros(dist.shape, np.float32)
        for window, dil in DILATION_GROUPS:
            total += (dist >= 0) & (dist <= window) & (dist % dil == 0)
        return total

    cache = count(CACHE_LEN + i - np.arange(CACHE_LEN)[None, :])
    new = count(i - np.arange(QB)[None, :]) * (np.arange(QB)[None, :] < DEC_SEQ)
    return cache, new.astype(np.float32)


def _sample_attention_step(q_ref, kn_ref, vn_ref, z_ref, kt_ref, vt_ref, cc_ref, cn_ref, o_ref):
    width = HEADS_PER_STEP * HD
    n_q = HEADS_PER_STEP * DEC_SEQ
    lane = lax.broadcasted_iota(jnp.int32, (n_q, width), 1)
    row = lax.broadcasted_iota(jnp.int32, (n_q, width), 0)
    own_head = (lane // HD) == (row // DEC_SEQ)
    q_rep = jnp.concatenate([q_ref[...]] * HEADS_PER_STEP, axis=0)
    q_big = jnp.where(own_head, q_rep, 0.0).astype(BF16)
    pad = jnp.zeros((QB - DEC_SEQ, width), F32)
    kn = jnp.concatenate([kn_ref[...], pad], axis=0).astype(BF16)
    vn = jnp.concatenate([vn_ref[...], pad], axis=0).astype(BF16)
    cc = cc_ref[...]
    cn = cn_ref[...]
    kt = kt_ref[...].reshape(width, CACHE_LEN).astype(BF16)
    s_c = jnp.where(cc > 0.0, _dot(q_big, kt), NEG)
    s_n = jnp.where(cn > 0.0, _dot_nt(q_big, kn), NEG)
    m = jnp.maximum(jnp.max(s_c, axis=1, keepdims=True), jnp.max(s_n, axis=1, keepdims=True))
    e_c = jnp.exp(s_c - m) * cc
    e_n = jnp.exp(s_n - m) * cn
    l = jnp.sum(e_c, axis=1, keepdims=True) + jnp.sum(e_n, axis=1, keepdims=True)
    vt = vt_ref[...].reshape(width, CACHE_LEN).astype(BF16)
    out_big = (_dot_nt(e_c.astype(BF16), vt) + _dot(e_n.astype(BF16), vn)) / l
    out_big = jnp.where(own_head, out_big, 0.0)
    out = jnp.sum(out_big.reshape(HEADS_PER_STEP, DEC_SEQ, width), axis=0)
    o_ref[...] = (out * z_ref[...]).astype(o_ref.dtype)


def _attn_kernel(qs_ref, kn_ref, vn_ref, zs_ref, kt_ref, vt_ref, cc_ref, cn_ref, *refs):
    n_g = len(DILATION_GROUPS)
    qkv_refs, (zp_ref, bias_ref, os_ref, op_ref), scratch = refs[:3 * n_g], refs[3 * n_g:3 * n_g + 4], refs[3 * n_g + 4:]
    _sample_attention_step(qs_ref, kn_ref, vn_ref, zs_ref, kt_ref, vt_ref, cc_ref, cn_ref, os_ref)
    _prompt_attention_step(pl.program_id(0) % STEPS_PER_UNIT, qkv_refs, zp_ref, bias_ref, op_ref,
                           scratch[:n_g], scratch[n_g:])


def _attn(proj, perms, kt_all, vt_all, e, cc, cn, bias):
    first = ROWS_P // DEC_SEQ
    width = HEADS_PER_STEP * HD
    per_row = H_B // HEADS_PER_STEP
    n_pairs = W_B // LANES
    n_steps = DEC_BATCH * per_row
    assert n_steps == BATCH * n_pairs * STEPS_PER_UNIT
    n_q = HEADS_PER_STEP * DEC_SEQ
    once = pl.Buffered(1)
    s_col = lambda c: (lambda s: (first + s // per_row, c * per_row + s % per_row))
    p_col = lambda c: (lambda s: (s // STEPS_PER_UNIT // n_pairs, c * n_pairs + s // STEPS_PER_UNIT % n_pairs))
    cache_spec = pl.BlockSpec((None, None, HEADS_PER_STEP, HD, CACHE_LEN),
                              lambda s: (e, s // per_row, s % per_row, 0, 0))
    perm_col = lambda c: (lambda s: (s // STEPS_PER_UNIT // n_pairs, 0, c * n_pairs + s // STEPS_PER_UNIT % n_pairs))
    perm_specs = [pl.BlockSpec((None, SEQ, LANES), perm_col(c), pipeline_mode=once) for c in range(3)]
    return pl.pallas_call(
        _attn_kernel,
        grid=(n_steps,),
        in_specs=[pl.BlockSpec((DEC_SEQ, width), s_col(0)),
                  pl.BlockSpec((DEC_SEQ, width), s_col(1)),
                  pl.BlockSpec((DEC_SEQ, width), s_col(2)),
                  pl.BlockSpec((DEC_SEQ, width), s_col(3)),
                  cache_spec, cache_spec,
                  pl.BlockSpec((n_q, CACHE_LEN), lambda s: (0, 0), pipeline_mode=once),
                  pl.BlockSpec((n_q, QB), lambda s: (0, 0), pipeline_mode=once)]
                 + perm_specs * len(perms)
                 + [pl.BlockSpec((SEQ, LANES), p_col(3), pipeline_mode=once),
                    pl.BlockSpec((3, QB, 2 * QB), lambda s: (0, 0, 0), pipeline_mode=once)],
        out_specs=[pl.BlockSpec((DEC_SEQ, width), lambda s: (s // per_row, s % per_row)),
                   pl.BlockSpec((SEQ, LANES), p_col(0))],
        out_shape=[jax.ShapeDtypeStruct((ROWS_S, W_B), BF16),
                   jax.ShapeDtypeStruct((ROWS_P, W_B), BF16)],
        scratch_shapes=[pltpu.VMEM((SEQ, LANES), F32)] * (2 * len(perms)),
        compiler_params=pltpu.CompilerParams(dimension_semantics=("arbitrary",),
                                             vmem_limit_bytes=ATTN_VMEM_LIMIT),
        name="attn",
    )(proj, proj, proj, proj, kt_all, vt_all, cc, cn, *[p for p in perms for _ in range(3)], proj, bias)


def _row_tiles(x, width, col=0):
    if not isinstance(x, tuple):
        return [pl.BlockSpec((TM, width), lambda i: (i, col))], [x]
    x_p, x_s = x
    return ([pl.BlockSpec((TM, width), lambda i: (jnp.minimum(i, N_TILES_P - 1), col)),
             pl.BlockSpec((TM, width), lambda i: (jnp.maximum(i - N_TILES_P, 0), col))],
            [x_p, x_s])


def _read_tile(refs):
    if len(refs) == 1:
        return refs[0][...]
    return jnp.where(pl.program_id(0) < N_TILES_P, refs[0][...], refs[1][...])


def _out_proj_kernel(n_b, n_h, a_ref, *refs):
    b_refs, refs = refs[:n_b], refs[n_b:]
    w_ref, refs = refs[0], refs[1:]
    h_refs, (g_ref, h_out_ref, n_out_ref) = refs[:n_h], refs[n_h:]
    half = w_ref.shape[0] // 2
    h_new = (_read_tile(h_refs) + _dot(a_ref[...], w_ref[0:half, :])
             + _dot(_read_tile(b_refs), w_ref[half:, :]))
    h_out_ref[...] = h_new
    ms = jnp.mean(h_new * h_new, axis=-1, keepdims=True)
    n_out_ref[...] = (h_new * lax.rsqrt(ms + EPS) * g_ref[...]).astype(n_out_ref.dtype)


def _out_proj(a, a_col, b, b_col, w, h, g, norm_dtype):
    half = w.shape[0] // 2
    b_specs, b_args = _row_tiles(b, half, b_col)
    h_specs, h_args = _row_tiles(h, D_MODEL)
    return pl.pallas_call(
        functools.partial(_out_proj_kernel, len(b_args), len(h_args)),
        grid=(N_TILES,),
        in_specs=[pl.BlockSpec((TM, half), lambda i: (i, a_col))] + b_specs
                 + [pl.BlockSpec((2 * half, D_MODEL), lambda i: (0, 0))] + h_specs
                 + [pl.BlockSpec((1, D_MODEL), lambda i: (0, 0))],
        out_specs=[pl.BlockSpec((TM, D_MODEL), lambda i: (i, 0)),
                   pl.BlockSpec((TM, D_MODEL), lambda i: (i, 0))],
        out_shape=[jax.ShapeDtypeStruct((ROWS, D_MODEL), F32),
                   jax.ShapeDtypeStruct((ROWS, D_MODEL), norm_dtype)],
        compiler_params=_params("arbitrary"),
        name="out_proj",
    )(a, *b_args, w, *h_args, g.reshape(1, D_MODEL))


TN_ODD = 512
N_COL_ODD = W_C // TN_ODD
PRE_SLOTS = BATCH + ROWS_S // TM


def _odd_in_kernel(x_ref, wb_ref, wc_ref, wx_ref, wz_ref, cw_ref, e1_ref, e2_ref, y_ref, pre_ref, buf):
    i = pl.program_id(1)

    @pl.when(i == 0)
    def _():
        buf[0:SUBLANES, :] = jnp.zeros((SUBLANES, TN_ODD), F32)

    x = x_ref[...]
    pre = _dot(x, wc_ref[...]) * _dot(x, wx_ref[...])
    pre_ref[...] = pre
    buf[SUBLANES:SUBLANES + TM, :] = pre
    row = lax.broadcasted_iota(jnp.int32, (TM, TN_ODD), 0)
    t = jnp.where(i < N_TILES_P, (i % TILES_PER_SEQ) * TM + row, row % DEC_SEQ)
    is_sample = i >= N_TILES_P
    pre_m1 = jnp.where(t >= 1, buf[SUBLANES - 1:SUBLANES - 1 + TM, :], jnp.where(is_sample, e1_ref[...], 0.0))
    pre_m2 = jnp.where(t >= 2, buf[SUBLANES - 2:SUBLANES - 2 + TM, :], jnp.where(is_sample, e2_ref[...], 0.0))
    conv = cw_ref[0:1, :] * pre_m2 + cw_ref[1:2, :] * pre_m1 + cw_ref[2:3, :] * pre
    y = _dot(x, wb_ref[...]) * conv * _silu(_dot(x, wz_ref[...]))
    y_ref[...] = y.astype(y_ref.dtype)
    buf[0:SUBLANES, :] = pre[TM - SUBLANES:, :]


def _odd_in(n, w, conv_w, e1, e2):
    wcol = lambda part: (lambda c, i: (0, part * N_COL_ODD + c))
    e_block = lambda c, i: (jnp.maximum(i - N_TILES_P, 0), c)
    pre_block = lambda c, i: (jnp.where(i < N_TILES_P, i // TILES_PER_SEQ, i - N_TILES_P + BATCH), c)
    return pl.pallas_call(
        _odd_in_kernel,
        grid=(N_COL_ODD, N_TILES),
        in_specs=[pl.BlockSpec((TM, D_MODEL), lambda c, i: (i, 0)),
                  pl.BlockSpec((D_MODEL, TN_ODD), wcol(0)),
                  pl.BlockSpec((D_MODEL, TN_ODD), wcol(1)),
                  pl.BlockSpec((D_MODEL, TN_ODD), wcol(2)),
                  pl.BlockSpec((D_MODEL, TN_ODD), wcol(3)),
                  pl.BlockSpec((CONV_W, TN_ODD), lambda c, i: (0, c)),
                  pl.BlockSpec((TM, TN_ODD), e_block),
                  pl.BlockSpec((TM, TN_ODD), e_block)],
        out_specs=[pl.BlockSpec((TM, TN_ODD), lambda c, i: (i, c)),
                   pl.BlockSpec((TM, TN_ODD), pre_block)],
        out_shape=[jax.ShapeDtypeStruct((ROWS, W_C), BF16),
                   jax.ShapeDtypeStruct((PRE_SLOTS * TM, W_C), F32)],
        scratch_shapes=[pltpu.VMEM((SUBLANES + TM, TN_ODD), F32)],
        compiler_params=_params("arbitrary", "arbitrary"),
        name="odd_in",
    )(n, w, w, w, w, conv_w, e1, e2)


def _rope_tables():
    half = HD // 2
    inv = 1.0 / (ROPE_THETA ** (jnp.arange(half, dtype=F32) / half))
    pos_s = PAST_LEN + (jnp.arange(TM_E) % DEC_SEQ)
    pos = jnp.concatenate([jnp.arange(SEQ), pos_s]).astype(F32)
    ang = pos[:, None] * inv[None, :]
    cos = jnp.cos(ang)
    sin = jnp.sin(ang)
    reps = LANES // HD
    return (jnp.tile(jnp.concatenate([cos, cos], axis=1), (1, reps)),
            jnp.tile(jnp.concatenate([-sin, sin], axis=1), (1, reps)))


def _gmlp_weights(w_s, b_s):
    tril = jnp.tril(jnp.ones((CHUNK, CHUNK), bool))
    w_p = jnp.where(tril[None], w_s, 0.0)
    small = w_p[:, :DEC_SEQ, :DEC_SEQ]
    eye = jnp.eye(CHUNK // DEC_SEQ, dtype=F32)
    w_smp = jnp.einsum('ab,hts->hatbs', eye, small).reshape(H_A, CHUNK, CHUNK)
    wm = jnp.stack([w_p, w_smp]).astype(BF16)
    bias_p = jnp.repeat(b_s.T, CH_A, axis=1)
    bias_s = jnp.tile(bias_p[:DEC_SEQ], (CHUNK // DEC_SEQ, 1))
    return wm, jnp.stack([bias_p, bias_s])


def _conv_edges(state):
    e1 = jnp.pad(state[:, CONV_W - 2:], ((0, 0), (0, DEC_SEQ - 1), (0, 0)))
    e2 = jnp.pad(state, ((0, 0), (0, DEC_SEQ - (CONV_W - 1)), (0, 0)))
    return e1.reshape(ROWS_S, W_C), e2.reshape(ROWS_S, W_C)


def kernel(x_prompt, x_sample, cache_b_k, cache_b_v, state_c_conv, norm_w, final_norm_w,
           w_in_even, w_s, b_s, w_out_even, w_in_odd, conv_w, w_out_odd):
    h = (x_prompt.reshape(ROWS_P, D_MODEL), x_sample.reshape(ROWS_S, D_MODEL))
    cos, sin = _rope_tables()
    band_bias = jnp.asarray(_band_bias())
    cc, cn = (jnp.asarray(c) for c in _sample_key_counts())
    kt_all = jnp.transpose(cache_b_k, (0, 1, 3, 4, 2))
    vt_all = jnp.transpose(cache_b_v, (0, 1, 3, 4, 2))
    n = _rmsnorm(h, norm_w[0])
    kp_l, vp_l, ks_l, vs_l, av_l, cp_l, cs_l = [], [], [], [], [], [], []
    buf_p = min(W_MAX, SEQ)
    for layer in range(DEPTH):
        last = layer == DEPTH - 1
        g_next = final_norm_w if last else norm_w[layer + 1]
        norm_dtype = F32 if last else BF16
        if layer % 2 == 0:
            e = layer // 2
            wm, bias = _gmlp_weights(w_s[e], b_s[e])
            a_out, av, proj, *perms = _even_in(n, w_in_even[e].astype(BF16), cos, sin, wm, bias)
            perms = [p.reshape(BATCH + 1, SEQ, 3 * W_B) for p in perms]
            att_s, att_p = _attn(proj, perms, kt_all, vt_all, e, cc, cn, band_bias)
            h, n = _out_proj(a_out, 0, (att_p, att_s), 0, w_out_even[e].astype(BF16), h, g_next, norm_dtype)
            for first_col, dst_p, dst_s in ((W_B, kp_l, ks_l), (2 * W_B, vp_l, vs_l)):
                for b in range(BATCH):
                    row0 = b * SEQ + SEQ - buf_p
                    dst_p.append(lax.slice(proj, (row0, first_col), (row0 + buf_p, first_col + W_B)))
                dst_s.append(lax.slice(proj, (ROWS_P, first_col), (ROWS, first_col + W_B)))
            av_l.append(av)
        else:
            c = layer // 2
            e1, e2 = _conv_edges(state_c_conv[c])
            y, pre = _odd_in(n, w_in_odd[c].astype(BF16), conv_w[c], e1, e2)
            h, n = _out_proj(y, 0, y, 1, w_out_odd[c].astype(BF16), h, g_next, norm_dtype)
            pre_p = pre[:BATCH * TM].reshape(BATCH, TM, W_C)
            cp_l.append(pre_p[:, TM - (CONV_W - 1):])
            cs_l.append(pre[BATCH * TM:].reshape(DEC_BATCH, DEC_SEQ, W_C)[:, DEC_SEQ - (CONV_W - 1):])
    y_prompt = n[:ROWS_P].reshape(BATCH, SEQ, D_MODEL)
    y_sample = n[ROWS_P:].reshape(DEC_BATCH, DEC_SEQ, D_MODEL)
    new_b_k_prompt = jnp.stack(kp_l).reshape(N_EVEN, BATCH, buf_p, H_B, HD)
    new_b_v_prompt = jnp.stack(vp_l).reshape(N_EVEN, BATCH, buf_p, H_B, HD)
    new_b_k_sample = jnp.stack(ks_l).reshape(N_EVEN, DEC_BATCH, DEC_SEQ, H_B, HD)
    new_b_v_sample = jnp.stack(vs_l).reshape(N_EVEN, DEC_BATCH, DEC_SEQ, H_B, HD)
    new_a_v_sample = jnp.stack(av_l).reshape(N_EVEN, DEC_BATCH, DEC_SEQ, W_A)
    return (y_prompt, y_sample, new_b_k_prompt, new_b_v_prompt, new_b_k_sample, new_b_v_sample,
            new_a_v_sample, jnp.stack(cp_l), jnp.stack(cs_l))
```

```python
import functools

import numpy as np
import jax
import jax.numpy as jnp
from jax import lax
from jax.experimental import pallas as pl
from jax.experimental.pallas import tpu as pltpu

D_MODEL = 1024
BATCH = 4
SEQ = 4096
DEPTH = 4
DEC_BATCH = 128
DEC_SEQ = 8
PAST_LEN = 2048
N_EVEN = (DEPTH + 1) // 2
N_ODD = DEPTH // 2
W_A = D_MODEL
CHUNK = 128
H_A = 8
CH_A = W_A // H_A
W_B = D_MODEL
H_B = 16
HD = W_B // H_B
DILATION_GROUPS = ((128, 1), (512, 4), (2048, 16))
W_MAX = 2048
ROPE_THETA = 10000.0
W_C = 2 * D_MODEL
CONV_W = 3
EPS = 1e-6
NEG = -1e30
EVEN_IN = 3 * W_A + 4 * W_B
ODD_IN = 4 * W_C

LANES = 128
SUBLANES = 8
ROWS_P = BATCH * SEQ
ROWS_S = DEC_BATCH * DEC_SEQ
ROWS = ROWS_P + ROWS_S
TM = 512
N_TILES = ROWS // TM
N_TILES_P = ROWS_P // TM
TILES_PER_SEQ = SEQ // TM
QB = 128
CACHE_LEN = min(W_MAX, PAST_LEN)
HEADS_PER_STEP = 8
N_PAIRS = W_B // LANES
PAIRS_PER_STEP = HEADS_PER_STEP * HD // LANES
Q, K, V, Z = range(4)
VMEM_LIMIT = 48 * 1024 * 1024
ATTN_VMEM_LIMIT = 56 * 1024 * 1024

assert all(window // dil == QB for window, dil in DILATION_GROUPS)

F32 = jnp.float32
BF16 = jnp.bfloat16


def _dot(a, b):
    return jnp.dot(a, b, preferred_element_type=F32)


def _dot_nt(a, b):
    return lax.dot_general(a, b, (((1,), (1,)), ((), ())), preferred_element_type=F32)


def _params(*sem):
    return pltpu.CompilerParams(dimension_semantics=sem, vmem_limit_bytes=VMEM_LIMIT)


def _rmsnorm_kernel(*refs):
    x_refs, (g_ref, o_ref) = refs[:-2], refs[-2:]
    x = _read_tile(x_refs)
    ms = jnp.mean(x * x, axis=-1, keepdims=True)
    o_ref[...] = (x * lax.rsqrt(ms + EPS) * g_ref[...]).astype(o_ref.dtype)


def _rmsnorm(x, g):
    x_specs, x_args = _row_tiles(x, D_MODEL)
    return pl.pallas_call(
        _rmsnorm_kernel,
        grid=(N_TILES,),
        in_specs=x_specs + [pl.BlockSpec((1, D_MODEL), lambda i: (0, 0))],
        out_specs=pl.BlockSpec((TM, D_MODEL), lambda i: (i, 0)),
        out_shape=jax.ShapeDtypeStruct((ROWS, D_MODEL), BF16),
        compiler_params=_params("arbitrary"),
        name="rmsnorm",
    )(*x_args, g.reshape(1, D_MODEL))


def _gelu(x):
    return 0.5 * x * (1.0 + lax.erf(x * np.float32(np.sqrt(0.5))))


def _silu(x):
    return x * jax.nn.sigmoid(x)


TM_E = 256
N_TILES_E = ROWS // TM_E
N_TILES_E_P = ROWS_P // TM_E


def _rope(x, cos, sin, o_ref, kind, scale):
    lane = lax.broadcasted_iota(jnp.int32, (TM_E, LANES), 1)
    first_half = (lane % HD) < (HD // 2)
    for p in range(N_PAIRS):
        xc = x[:, p * LANES:(p + 1) * LANES]
        partner = jnp.where(first_half, pltpu.roll(xc, LANES - HD // 2, 1), pltpu.roll(xc, HD // 2, 1))
        out = xc * cos + partner * sin
        if scale is not None:
            out = out * scale
        o_ref[p, kind] = out


def _even_in_kernel(x_ref, w_ref, cos_ref, sin_ref, wm_ref, bias_ref, a_ref, av_ref, o_ref):
    x = x_ref[...]
    col = lambda j: w_ref[:, j * D_MODEL:(j + 1) * D_MODEL]
    v = _gelu(_dot(x, col(1)))
    av_ref[...] = v
    v_bf = v.astype(BF16)
    u = _gelu(_dot(x, col(0)))
    z = _silu(_dot(x, col(2)))
    for c in range(TM_E // CHUNK):
        rows = slice(c * CHUNK, (c + 1) * CHUNK)
        for h in range(H_A):
            sl = slice(h * CH_A, (h + 1) * CH_A)
            mix = _dot(wm_ref[h], v_bf[rows, sl]) + bias_ref[:, sl]
            a_ref[rows, sl] = (u[rows, sl] * mix * z[rows, sl]).astype(a_ref.dtype)
    cos = cos_ref[...]
    sin = sin_ref[...]
    _rope(_dot(x, col(3)), cos, sin, o_ref, Q, np.float32(HD ** -0.5))
    _rope(_dot(x, col(4)), cos, sin, o_ref, K, None)
    v_b = _dot(x, col(5))
    z_b = _silu(_dot(x, col(6)))
    for p in range(N_PAIRS):
        o_ref[p, V] = v_b[:, p * LANES:(p + 1) * LANES]
        o_ref[p, Z] = z_b[:, p * LANES:(p + 1) * LANES]


def _even_in(n, w, cos, sin, wm, bias):
    table = lambda i: (jnp.where(i < N_TILES_E_P, i % (SEQ // TM_E), SEQ // TM_E), 0)
    variant = lambda i: jnp.where(i < N_TILES_E_P, 0, 1)
    return pl.pallas_call(
        _even_in_kernel,
        grid=(N_TILES_E,),
        in_specs=[pl.BlockSpec((TM_E, D_MODEL), lambda i: (i, 0)),
                  pl.BlockSpec((D_MODEL, EVEN_IN), lambda i: (0, 0), pipeline_mode=pl.Buffered(1)),
                  pl.BlockSpec((TM_E, LANES), table),
                  pl.BlockSpec((TM_E, LANES), table),
                  pl.BlockSpec((None, H_A, CHUNK, CHUNK), lambda i: (variant(i), 0, 0, 0)),
                  pl.BlockSpec((None, CHUNK, W_A), lambda i: (variant(i), 0, 0))],
        out_specs=[pl.BlockSpec((TM_E, W_A), lambda i: (i, 0)),
                   pl.BlockSpec((TM_E, W_A), lambda i: (jnp.maximum(i - N_TILES_E_P, 0), 0)),
                   pl.BlockSpec((N_PAIRS, 4, TM_E, LANES), lambda i: (0, 0, i, 0))],
        out_shape=[jax.ShapeDtypeStruct((ROWS, W_A), BF16),
                   jax.ShapeDtypeStruct((ROWS_S, W_A), F32),
                   jax.ShapeDtypeStruct((N_PAIRS, 4, ROWS, LANES), F32)],
        compiler_params=_params("arbitrary"),
        name="even_in",
    )(n, w, cos, sin, wm, bias)


def _band_bias():
    i = np.arange(QB)[:, None]
    j = np.arange(2 * QB)[None, :]
    band = (j - i >= 0) & (j - i <= QB)
    first = band & (j >= QB)
    return np.where(np.stack([band, first]), 0.0, NEG).astype(np.float32)


STEPS_PER_UNIT = 8
HALF_BLOCKS = SEQ // QB // 2


def _prompt_attention_step(phase, x_ref, bias_ref, o_ref,
                           q0_s, q1_s, k_s, v_s, o0_s, o1_s, o2_s, lse0_s, lse1_s, lse2_s):
    o_refs = (o0_s, o1_s, o2_s)
    lse_refs = (lse0_s, lse1_s, lse2_s)
    head0 = lax.broadcasted_iota(jnp.int32, (QB, LANES), 1) < HD
    ones = jnp.ones((2 * QB, LANES), BF16)

    @pl.when(phase == 0)
    def _():
        k_s[0:QB, :] = jnp.zeros((QB, LANES), BF16)
        v_s[0:QB, :] = jnp.zeros((QB, LANES), BF16)

    for g, (_, dil) in enumerate(DILATION_GROUPS):
        n = SEQ // dil
        ch = min(n, 4 * QB)
        n_ch = n // ch
        n_blk = n // QB

        def deinterleave(idx, carry, dil=dil, ch=ch, n_ch=n_ch):
            start = idx // n_ch + dil * ch * (idx % n_ch)
            src = pl.ds(pl.multiple_of(start, ch), ch) if dil == 1 else pl.ds(start, ch, stride=dil)
            dst = pl.ds(pl.multiple_of(idx * ch, QB), ch)
            dst_pad = pl.ds(pl.multiple_of(QB + idx * ch, QB), ch)
            first = lax.broadcasted_iota(jnp.int32, (ch, LANES), 1) < HD
            q = x_ref[Q, src, :]
            q0_s[dst, :] = jnp.where(first, q, 0.0).astype(BF16)
            q1_s[dst, :] = jnp.where(first, 0.0, q).astype(BF16)
            k_s[dst_pad, :] = x_ref[K, src, :].astype(BF16)
            v_s[dst_pad, :] = x_ref[V, src, :].astype(BF16)
            return carry

        @pl.when(phase == 2 * g)
        def _(ch=ch, deinterleave=deinterleave):
            lax.fori_loop(0, SEQ // ch, deinterleave, 0)

        def body(i, carry, g=g, dil=dil, n_blk=n_blk):
            blk = (phase % 2) * HALF_BLOCKS + i
            res = blk // n_blk
            pos = blk % n_blk
            rows = pl.ds(pl.multiple_of(blk * QB, QB), QB)
            rows2 = pl.ds(pl.multiple_of(blk * QB, QB), 2 * QB)
            k2 = k_s[rows2, :]
            v_aug = jnp.concatenate([v_s[rows2, :], ones], axis=1)
            bias = bias_ref[jnp.where(pos > 0, 0, 1)]
            q2 = jnp.concatenate([q0_s[rows, :], q1_s[rows, :]], axis=0)
            s = _dot_nt(q2, k2) + jnp.concatenate([bias, bias], axis=0)
            m = jnp.max(s, axis=1, keepdims=True)
            r = _dot(jnp.exp(s - m).astype(BF16), v_aug)
            r = jnp.where(jnp.concatenate([head0, head0], axis=1), r[:QB], r[QB:])
            l = r[:, LANES:]
            start = res + dil * QB * pos
            dst = pl.ds(pl.multiple_of(start, QB), QB) if dil == 1 else pl.ds(start, QB, stride=dil)
            o_refs[g][dst, :] = r[:, :LANES] / l
            lse_refs[g][dst, :] = jnp.where(head0, m[:QB], m[QB:]) + jnp.log(l)
            return carry

        @pl.when(phase // 2 == g)
        def _(body=body):
            lax.fori_loop(0, HALF_BLOCKS, body, 0, unroll=16)

    def finish(i, carry):
        blk = (phase % 2) * HALF_BLOCKS + i
        rows = pl.ds(pl.multiple_of(blk * QB, QB), QB)
        lses = [ref[rows, :] for ref in lse_refs]
        top = jnp.maximum(jnp.maximum(lses[0], lses[1]), lses[2])
        wts = [jnp.exp(lse - top) for lse in lses]
        num = wts[0] * o0_s[rows, :] + wts[1] * o1_s[rows, :] + wts[2] * o2_s[rows, :]
        den = wts[0] + wts[1] + wts[2]
        o_ref[rows, :] = (num / den * x_ref[Z, rows, :]).astype(o_ref.dtype)
        return carry

    @pl.when(phase // 2 == len(DILATION_GROUPS))
    def _():
        lax.fori_loop(0, HALF_BLOCKS, finish, 0)


def _sample_key_counts():
    i = (np.arange(HEADS_PER_STEP * DEC_SEQ) % DEC_SEQ)[:, None]

    def count(dist):
        total = np.zeros(dist.shape, np.float32)
        for window, dil in DILATION_GROUPS:
            total += (dist >= 0) & (dist <= window) & (dist % dil == 0)
        return total

    cache = count(CACHE_LEN + i - np.arange(CACHE_LEN)[None, :])
    new = count(i - np.arange(QB)[None, :]) * (np.arange(QB)[None, :] < DEC_SEQ)
    return cache, new.astype(np.float32)


def _sample_attention_step(x_ref, kt_ref, vt_ref, cc_ref, cn_ref, o_ref):
    width = HEADS_PER_STEP * HD
    n_q = HEADS_PER_STEP * DEC_SEQ
    wide = lambda kind: jnp.concatenate([x_ref[p, kind] for p in range(PAIRS_PER_STEP)], axis=1)
    lane = lax.broadcasted_iota(jnp.int32, (n_q, width), 1)
    row = lax.broadcasted_iota(jnp.int32, (n_q, width), 0)
    own_head = (lane // HD) == (row // DEC_SEQ)
    q_rep = jnp.concatenate([wide(Q)] * HEADS_PER_STEP, axis=0)
    q_big = jnp.where(own_head, q_rep, 0.0).astype(BF16)
    pad = jnp.zeros((QB - DEC_SEQ, width), F32)
    kn = jnp.concatenate([wide(K), pad], axis=0).astype(BF16)
    vn = jnp.concatenate([wide(V), pad], axis=0).astype(BF16)
    cc = cc_ref[...]
    cn = cn_ref[...]
    kt = kt_ref[...].reshape(width, CACHE_LEN).astype(BF16)
    s_c = jnp.where(cc > 0.0, _dot(q_big, kt), NEG)
    s_n = jnp.where(cn > 0.0, _dot_nt(q_big, kn), NEG)
    m = jnp.maximum(jnp.max(s_c, axis=1, keepdims=True), jnp.max(s_n, axis=1, keepdims=True))
    e_c = jnp.exp(s_c - m) * cc
    e_n = jnp.exp(s_n - m) * cn
    l = jnp.sum(e_c, axis=1, keepdims=True) + jnp.sum(e_n, axis=1, keepdims=True)
    vt = vt_ref[...].reshape(width, CACHE_LEN).astype(BF16)
    out_big = (_dot_nt(e_c.astype(BF16), vt) + _dot(e_n.astype(BF16), vn)) / l
    out_big = jnp.where(own_head, out_big, 0.0)
    out = jnp.sum(out_big.reshape(HEADS_PER_STEP, DEC_SEQ, width), axis=0)
    o_ref[...] = (out * wide(Z)).astype(o_ref.dtype)


def _attn_kernel(xs_ref, kt_ref, vt_ref, cc_ref, cn_ref, xp_ref, bias_ref, os_ref, op_ref, *scratch):
    _sample_attention_step(xs_ref, kt_ref, vt_ref, cc_ref, cn_ref, os_ref)
    _prompt_attention_step(pl.program_id(0) % STEPS_PER_UNIT, xp_ref, bias_ref, op_ref, *scratch)


def _attn(x, kt_all, vt_all, e, cc, cn, bias):
    first = ROWS_P // DEC_SEQ
    width = HEADS_PER_STEP * HD
    per_row = H_B // HEADS_PER_STEP
    n_steps = DEC_BATCH * per_row
    assert n_steps == BATCH * N_PAIRS * STEPS_PER_UNIT
    n_q = HEADS_PER_STEP * DEC_SEQ
    once = pl.Buffered(1)
    unit = lambda s: s // STEPS_PER_UNIT
    cache_spec = pl.BlockSpec((None, None, HEADS_PER_STEP, HD, CACHE_LEN),
                              lambda s: (e, s // per_row, s % per_row, 0, 0))
    return pl.pallas_call(
        _attn_kernel,
        grid=(n_steps,),
        in_specs=[pl.BlockSpec((PAIRS_PER_STEP, 4, DEC_SEQ, LANES), lambda s: (s % per_row, 0, first + s // per_row, 0)),
                  cache_spec, cache_spec,
                  pl.BlockSpec((n_q, CACHE_LEN), lambda s: (0, 0), pipeline_mode=once),
                  pl.BlockSpec((n_q, QB), lambda s: (0, 0), pipeline_mode=once),
                  pl.BlockSpec((None, 4, SEQ, LANES), lambda s: (unit(s) % N_PAIRS, 0, unit(s) // N_PAIRS, 0),
                               pipeline_mode=once),
                  pl.BlockSpec((2, QB, 2 * QB), lambda s: (0, 0, 0), pipeline_mode=once)],
        out_specs=[pl.BlockSpec((DEC_SEQ, width), lambda s: (s // per_row, s % per_row)),
                   pl.BlockSpec((SEQ, LANES), lambda s: (unit(s) // N_PAIRS, unit(s) % N_PAIRS))],
        out_shape=[jax.ShapeDtypeStruct((ROWS_S, W_B), BF16),
                   jax.ShapeDtypeStruct((ROWS_P, W_B), BF16)],
        scratch_shapes=[pltpu.VMEM((SEQ, LANES), BF16)] * 2
                       + [pltpu.VMEM((QB + SEQ, LANES), BF16)] * 2
                       + [pltpu.VMEM((SEQ, LANES), F32)] * 6,
        compiler_params=pltpu.CompilerParams(dimension_semantics=("arbitrary",),
                                             vmem_limit_bytes=ATTN_VMEM_LIMIT),
        name="attn",
    )(x, kt_all, vt_all, cc, cn, x, bias)


def _row_tiles(x, width, col=0):
    if not isinstance(x, tuple):
        return [pl.BlockSpec((TM, width), lambda i: (i, col))], [x]
    x_p, x_s = x
    return ([pl.BlockSpec((TM, width), lambda i: (jnp.minimum(i, N_TILES_P - 1), col)),
             pl.BlockSpec((TM, width), lambda i: (jnp.maximum(i - N_TILES_P, 0), col))],
            [x_p, x_s])


def _read_tile(refs):
    if len(refs) == 1:
        return refs[0][...]
    return jnp.where(pl.program_id(0) < N_TILES_P, refs[0][...], refs[1][...])


def _out_proj_kernel(n_b, n_h, a_ref, *refs):
    b_refs, refs = refs[:n_b], refs[n_b:]
    w_ref, refs = refs[0], refs[1:]
    h_refs, (g_ref, h_out_ref, n_out_ref) = refs[:n_h], refs[n_h:]
    half = w_ref.shape[0] // 2
    h_new = (_read_tile(h_refs) + _dot(a_ref[...], w_ref[0:half, :])
             + _dot(_read_tile(b_refs), w_ref[half:, :]))
    h_out_ref[...] = h_new
    ms = jnp.mean(h_new * h_new, axis=-1, keepdims=True)
    n_out_ref[...] = (h_new * lax.rsqrt(ms + EPS) * g_ref[...]).astype(n_out_ref.dtype)


def _out_proj(a, a_col, b, b_col, w, h, g, norm_dtype):
    half = w.shape[0] // 2
    b_specs, b_args = _row_tiles(b, half, b_col)
    h_specs, h_args = _row_tiles(h, D_MODEL)
    return pl.pallas_call(
        functools.partial(_out_proj_kernel, len(b_args), len(h_args)),
        grid=(N_TILES,),
        in_specs=[pl.BlockSpec((TM, half), lambda i: (i, a_col))] + b_specs
                 + [pl.BlockSpec((2 * half, D_MODEL), lambda i: (0, 0))] + h_specs
                 + [pl.BlockSpec((1, D_MODEL), lambda i: (0, 0))],
        out_specs=[pl.BlockSpec((TM, D_MODEL), lambda i: (i, 0)),
                   pl.BlockSpec((TM, D_MODEL), lambda i: (i, 0))],
        out_shape=[jax.ShapeDtypeStruct((ROWS, D_MODEL), F32),
                   jax.ShapeDtypeStruct((ROWS, D_MODEL), norm_dtype)],
        compiler_params=_params("arbitrary"),
        name="out_proj",
    )(a, *b_args, w, *h_args, g.reshape(1, D_MODEL))


TN_ODD = 512
N_COL_ODD = W_C // TN_ODD
PRE_SLOTS = BATCH + ROWS_S // TM


def _odd_in_kernel(x_ref, wb_ref, wc_ref, wx_ref, wz_ref, cw_ref, e1_ref, e2_ref, y_ref, pre_ref, buf):
    i = pl.program_id(1)

    @pl.when(i == 0)
    def _():
        buf[0:SUBLANES, :] = jnp.zeros((SUBLANES, TN_ODD), F32)

    x = x_ref[...]
    pre = _dot(x, wc_ref[...]) * _dot(x, wx_ref[...])
    pre_ref[...] = pre
    buf[SUBLANES:SUBLANES + TM, :] = pre
    row = lax.broadcasted_iota(jnp.int32, (TM, TN_ODD), 0)
    t = jnp.where(i < N_TILES_P, (i % TILES_PER_SEQ) * TM + row, row % DEC_SEQ)
    is_sample = i >= N_TILES_P
    pre_m1 = jnp.where(t >= 1, buf[SUBLANES - 1:SUBLANES - 1 + TM, :], jnp.where(is_sample, e1_ref[...], 0.0))
    pre_m2 = jnp.where(t >= 2, buf[SUBLANES - 2:SUBLANES - 2 + TM, :], jnp.where(is_sample, e2_ref[...], 0.0))
    conv = cw_ref[0:1, :] * pre_m2 + cw_ref[1:2, :] * pre_m1 + cw_ref[2:3, :] * pre
    y = _dot(x, wb_ref[...]) * conv * _silu(_dot(x, wz_ref[...]))
    y_ref[...] = y.astype(y_ref.dtype)
    buf[0:SUBLANES, :] = pre[TM - SUBLANES:, :]


def _odd_in(n, w, conv_w, e1, e2):
    wcol = lambda part: (lambda c, i: (0, part * N_COL_ODD + c))
    e_block = lambda c, i: (jnp.maximum(i - N_TILES_P, 0), c)
    pre_block = lambda c, i: (jnp.where(i < N_TILES_P, i // TILES_PER_SEQ, i - N_TILES_P + BATCH), c)
    return pl.pallas_call(
        _odd_in_kernel,
        grid=(N_COL_ODD, N_TILES),
        in_specs=[pl.BlockSpec((TM, D_MODEL), lambda c, i: (i, 0)),
                  pl.BlockSpec((D_MODEL, TN_ODD), wcol(0)),
                  pl.BlockSpec((D_MODEL, TN_ODD), wcol(1)),
                  pl.BlockSpec((D_MODEL, TN_ODD), wcol(2)),
                  pl.BlockSpec((D_MODEL, TN_ODD), wcol(3)),
                  pl.BlockSpec((CONV_W, TN_ODD), lambda c, i: (0, c)),
                  pl.BlockSpec((TM, TN_ODD), e_block),
                  pl.BlockSpec((TM, TN_ODD), e_block)],
        out_specs=[pl.BlockSpec((TM, TN_ODD), lambda c, i: (i, c)),
                   pl.BlockSpec((TM, TN_ODD), pre_block)],
        out_shape=[jax.ShapeDtypeStruct((ROWS, W_C), BF16),
                   jax.ShapeDtypeStruct((PRE_SLOTS * TM, W_C), F32)],
        scratch_shapes=[pltpu.VMEM((SUBLANES + TM, TN_ODD), F32)],
        compiler_params=_params("arbitrary", "arbitrary"),
        name="odd_in",
    )(n, w, w, w, w, conv_w, e1, e2)


def _rope_tables():
    half = HD // 2
    inv = 1.0 / (ROPE_THETA ** (jnp.arange(half, dtype=F32) / half))
    pos_s = PAST_LEN + (jnp.arange(TM_E) % DEC_SEQ)
    pos = jnp.concatenate([jnp.arange(SEQ), pos_s]).astype(F32)
    ang = pos[:, None] * inv[None, :]
    cos = jnp.cos(ang)
    sin = jnp.sin(ang)
    reps = LANES // HD
    return (jnp.tile(jnp.concatenate([cos, cos], axis=1), (1, reps)),
            jnp.tile(jnp.concatenate([-sin, sin], axis=1), (1, reps)))


def _gmlp_weights(w_s, b_s):
    tril = jnp.tril(jnp.ones((CHUNK, CHUNK), bool))
    w_p = jnp.where(tril[None], w_s, 0.0)
    small = w_p[:, :DEC_SEQ, :DEC_SEQ]
    eye = jnp.eye(CHUNK // DEC_SEQ, dtype=F32)
    w_smp = jnp.einsum('ab,hts->hatbs', eye, small).reshape(H_A, CHUNK, CHUNK)
    wm = jnp.stack([w_p, w_smp]).astype(BF16)
    bias_p = jnp.repeat(b_s.T, CH_A, axis=1)
    bias_s = jnp.tile(bias_p[:DEC_SEQ], (CHUNK // DEC_SEQ, 1))
    return wm, jnp.stack([bias_p, bias_s])


def _conv_edges(state):
    e1 = jnp.pad(state[:, CONV_W - 2:], ((0, 0), (0, DEC_SEQ - 1), (0, 0)))
    e2 = jnp.pad(state, ((0, 0), (0, DEC_SEQ - (CONV_W - 1)), (0, 0)))
    return e1.reshape(ROWS_S, W_C), e2.reshape(ROWS_S, W_C)


def _rows_of(x, kind, row0, n_rows):
    part = lax.slice(x, (0, kind, row0, 0), (N_PAIRS, kind + 1, row0 + n_rows, LANES))
    return jnp.transpose(part.reshape(N_PAIRS, n_rows, LANES), (1, 0, 2)).reshape(n_rows, W_B)


def kernel(x_prompt, x_sample, cache_b_k, cache_b_v, state_c_conv, norm_w, final_norm_w,
           w_in_even, w_s, b_s, w_out_even, w_in_odd, conv_w, w_out_odd):
    h = (x_prompt.reshape(ROWS_P, D_MODEL), x_sample.reshape(ROWS_S, D_MODEL))
    cos, sin = _rope_tables()
    band_bias = jnp.asarray(_band_bias())
    cc, cn = (jnp.asarray(c) for c in _sample_key_counts())
    kt_all = jnp.transpose(cache_b_k, (0, 1, 3, 4, 2))
    vt_all = jnp.transpose(cache_b_v, (0, 1, 3, 4, 2))
    n = _rmsnorm(h, norm_w[0])
    kp_l, vp_l, ks_l, vs_l, av_l, cp_l, cs_l = [], [], [], [], [], [], []
    buf_p = min(W_MAX, SEQ)
    for layer in range(DEPTH):
        last = layer == DEPTH - 1
        g_next = final_norm_w if last else norm_w[layer + 1]
        norm_dtype = F32 if last else BF16
        if layer % 2 == 0:
            e = layer // 2
            wm, bias = _gmlp_weights(w_s[e], b_s[e])
            a_out, av, att_in = _even_in(n, w_in_even[e].astype(BF16), cos, sin, wm, bias)
            att_s, att_p = _attn(att_in, kt_all, vt_all, e, cc, cn, band_bias)
            h, n = _out_proj(a_out, 0, (att_p, att_s), 0, w_out_even[e].astype(BF16), h, g_next, norm_dtype)
            for kind, dst_p, dst_s in ((K, kp_l, ks_l), (V, vp_l, vs_l)):
                for b in range(BATCH):
                    dst_p.append(_rows_of(att_in, kind, b * SEQ + SEQ - buf_p, buf_p))
                dst_s.append(_rows_of(att_in, kind, ROWS_P, ROWS_S))
            av_l.append(av)
        else:
            c = layer // 2
            e1, e2 = _conv_edges(state_c_conv[c])
            y, pre = _odd_in(n, w_in_odd[c].astype(BF16), conv_w[c], e1, e2)
            h, n = _out_proj(y, 0, y, 1, w_out_odd[c].astype(BF16), h, g_next, norm_dtype)
            pre_p = pre[:BATCH * TM].reshape(BATCH, TM, W_C)
            cp_l.append(pre_p[:, TM - (CONV_W - 1):])
            cs_l.append(pre[BATCH * TM:].reshape(DEC_BATCH, DEC_SEQ, W_C)[:, DEC_SEQ - (CONV_W - 1):])
    y_prompt = n[:ROWS_P].reshape(BATCH, SEQ, D_MODEL)
    y_sample = n[ROWS_P:].reshape(DEC_BATCH, DEC_SEQ, D_MODEL)
    new_b_k_prompt = jnp.stack(kp_l).reshape(N_EVEN, BATCH, buf_p, H_B, HD)
    new_b_v_prompt = jnp.stack(vp_l).reshape(N_EVEN, BATCH, buf_p, H_B, HD)
    new_b_k_sample = jnp.stack(ks_l).reshape(N_EVEN, DEC_BATCH, DEC_SEQ, H_B, HD)
    new_b_v_sample = jnp.stack(vs_l).reshape(N_EVEN, DEC_BATCH, DEC_SEQ, H_B, HD)
    new_a_v_sample = jnp.stack(av_l).reshape(N_EVEN, DEC_BATCH, DEC_SEQ, W_A)
    return (y_prompt, y_sample, new_b_k_prompt, new_b_v_prompt, new_b_k_sample, new_b_v_sample,
            new_a_v_sample, jnp.stack(cp_l), jnp.stack(cs_l))
```

```python
import functools

import numpy as np
import jax
import jax.numpy as jnp
from jax import lax
from jax.experimental import pallas as pl
from jax.experimental.pallas import tpu as pltpu

D_MODEL = 1024
BATCH = 4
SEQ = 4096
DEPTH = 4
DEC_BATCH = 128
DEC_SEQ = 8
PAST_LEN = 2048
N_EVEN = (DEPTH + 1) // 2
N_ODD = DEPTH // 2
W_A = D_MODEL
CHUNK = 128
H_A = 8
CH_A = W_A // H_A
W_B = D_MODEL
H_B = 16
HD = W_B // H_B
DILATION_GROUPS = ((128, 1), (512, 4), (2048, 16))
W_MAX = 2048
ROPE_THETA = 10000.0
W_C = 2 * D_MODEL
CONV_W = 3
EPS = 1e-6
NEG = -1e30
EVEN_IN = 3 * W_A + 4 * W_B
ODD_IN = 4 * W_C

LANES = 128
SUBLANES = 8
ROWS_P = BATCH * SEQ
ROWS_S = DEC_BATCH * DEC_SEQ
ROWS = ROWS_P + ROWS_S
TM = 512
N_TILES = ROWS // TM
N_TILES_P = ROWS_P // TM
TILES_PER_SEQ = SEQ // TM
QB = 128
CACHE_LEN = min(W_MAX, PAST_LEN)
HEADS_PER_STEP = 8
N_PAIRS = W_B // LANES
PAIRS_PER_STEP = HEADS_PER_STEP * HD // LANES
Q, K, V, Z = range(4)
VMEM_LIMIT = 48 * 1024 * 1024
ATTN_VMEM_LIMIT = 60 * 1024 * 1024

assert all(window // dil == QB for window, dil in DILATION_GROUPS)

F32 = jnp.float32
BF16 = jnp.bfloat16


def _dot(a, b):
    return jnp.dot(a, b, preferred_element_type=F32)


def _dot_nt(a, b):
    return lax.dot_general(a, b, (((1,), (1,)), ((), ())), preferred_element_type=F32)


def _params(*sem):
    return pltpu.CompilerParams(dimension_semantics=sem, vmem_limit_bytes=VMEM_LIMIT)


def _rmsnorm_kernel(*refs):
    x_refs, (g_ref, o_ref) = refs[:-2], refs[-2:]
    x = _read_tile(x_refs)
    ms = jnp.mean(x * x, axis=-1, keepdims=True)
    o_ref[...] = (x * lax.rsqrt(ms + EPS) * g_ref[...]).astype(o_ref.dtype)


def _rmsnorm(x, g):
    x_specs, x_args = _row_tiles(x, D_MODEL)
    return pl.pallas_call(
        _rmsnorm_kernel,
        grid=(N_TILES,),
        in_specs=x_specs + [pl.BlockSpec((1, D_MODEL), lambda i: (0, 0))],
        out_specs=pl.BlockSpec((TM, D_MODEL), lambda i: (i, 0)),
        out_shape=jax.ShapeDtypeStruct((ROWS, D_MODEL), BF16),
        compiler_params=_params("arbitrary"),
        name="rmsnorm",
    )(*x_args, g.reshape(1, D_MODEL))


def _gelu(x):
    return 0.5 * x * (1.0 + lax.erf(x * np.float32(np.sqrt(0.5))))


def _silu(x):
    return x * jax.nn.sigmoid(x)


TM_E = 256
N_TILES_E = ROWS // TM_E
N_TILES_E_P = ROWS_P // TM_E


def _rope(x, cos, sin, o_ref, kind, scale):
    lane = lax.broadcasted_iota(jnp.int32, (TM_E, LANES), 1)
    first_half = (lane % HD) < (HD // 2)
    for p in range(N_PAIRS):
        xc = x[:, p * LANES:(p + 1) * LANES]
        partner = jnp.where(first_half, pltpu.roll(xc, LANES - HD // 2, 1), pltpu.roll(xc, HD // 2, 1))
        out = xc * cos + partner * sin
        if scale is not None:
            out = out * scale
        o_ref[p, kind] = out


def _even_in_kernel(x_ref, w_ref, cos_ref, sin_ref, wm_ref, bias_ref, a_ref, av_ref, o_ref):
    x = x_ref[...]
    col = lambda j: w_ref[:, j * D_MODEL:(j + 1) * D_MODEL]
    v = _gelu(_dot(x, col(1)))
    av_ref[...] = v
    v_bf = v.astype(BF16)
    u = _gelu(_dot(x, col(0)))
    z = _silu(_dot(x, col(2)))
    for c in range(TM_E // CHUNK):
        rows = slice(c * CHUNK, (c + 1) * CHUNK)
        for h in range(H_A):
            sl = slice(h * CH_A, (h + 1) * CH_A)
            mix = _dot(wm_ref[h], v_bf[rows, sl]) + bias_ref[:, sl]
            a_ref[rows, sl] = (u[rows, sl] * mix * z[rows, sl]).astype(a_ref.dtype)
    cos = cos_ref[...]
    sin = sin_ref[...]
    _rope(_dot(x, col(3)), cos, sin, o_ref, Q, np.float32(HD ** -0.5))
    _rope(_dot(x, col(4)), cos, sin, o_ref, K, None)
    v_b = _dot(x, col(5))
    z_b = _silu(_dot(x, col(6)))
    for p in range(N_PAIRS):
        o_ref[p, V] = v_b[:, p * LANES:(p + 1) * LANES]
        o_ref[p, Z] = z_b[:, p * LANES:(p + 1) * LANES]


def _even_in(n, w, cos, sin, wm, bias):
    table = lambda i: (jnp.where(i < N_TILES_E_P, i % (SEQ // TM_E), SEQ // TM_E), 0)
    variant = lambda i: jnp.where(i < N_TILES_E_P, 0, 1)
    return pl.pallas_call(
        _even_in_kernel,
        grid=(N_TILES_E,),
        in_specs=[pl.BlockSpec((TM_E, D_MODEL), lambda i: (i, 0)),
                  pl.BlockSpec((D_MODEL, EVEN_IN), lambda i: (0, 0), pipeline_mode=pl.Buffered(1)),
                  pl.BlockSpec((TM_E, LANES), table),
                  pl.BlockSpec((TM_E, LANES), table),
                  pl.BlockSpec((None, H_A, CHUNK, CHUNK), lambda i: (variant(i), 0, 0, 0)),
                  pl.BlockSpec((None, CHUNK, W_A), lambda i: (variant(i), 0, 0))],
        out_specs=[pl.BlockSpec((TM_E, W_A), lambda i: (i, 0)),
                   pl.BlockSpec((TM_E, W_A), lambda i: (jnp.maximum(i - N_TILES_E_P, 0), 0)),
                   pl.BlockSpec((N_PAIRS, 4, TM_E, LANES), lambda i: (0, 0, i, 0))],
        out_shape=[jax.ShapeDtypeStruct((ROWS, W_A), BF16),
                   jax.ShapeDtypeStruct((ROWS_S, W_A), F32),
                   jax.ShapeDtypeStruct((N_PAIRS, 4, ROWS, LANES), F32)],
        compiler_params=_params("arbitrary"),
        name="even_in",
    )(n, w, cos, sin, wm, bias)


def _band_bias():
    i = np.arange(QB)[:, None]
    j = np.arange(2 * QB)[None, :]
    band = (j - i >= 0) & (j - i <= QB)
    first = band & (j >= QB)
    return np.where(np.stack([band, first]), 0.0, NEG).astype(np.float32)


STEPS_PER_UNIT = 8
HALF_BLOCKS = SEQ // QB // 2


def _prompt_attention_step(phase, x_ref, bias_ref, o_ref,
                           q0_s, q1_s, k_s, v_s, o0_s, o1_s, o2_s, lse0_s, lse1_s, lse2_s):
    o_refs = (o0_s, o1_s, o2_s)
    lse_refs = (lse0_s, lse1_s, lse2_s)
    head0 = lax.broadcasted_iota(jnp.int32, (QB, LANES), 1) < HD
    ones = jnp.ones((2 * QB, LANES), BF16)

    @pl.when(phase == 0)
    def _():
        k_s[0:QB, :] = jnp.zeros((QB, LANES), BF16)
        v_s[0:QB, :] = jnp.zeros((QB, LANES), BF16)

    for g, (_, dil) in enumerate(DILATION_GROUPS):
        n = SEQ // dil
        ch = min(n, 4 * QB)
        n_ch = n // ch
        n_blk = n // QB

        def deinterleave(idx, carry, dil=dil, ch=ch, n_ch=n_ch):
            start = idx // n_ch + dil * ch * (idx % n_ch)
            src = pl.ds(pl.multiple_of(start, ch), ch) if dil == 1 else pl.ds(start, ch, stride=dil)
            dst = pl.ds(pl.multiple_of(idx * ch, QB), ch)
            dst_pad = pl.ds(pl.multiple_of(QB + idx * ch, QB), ch)
            first = lax.broadcasted_iota(jnp.int32, (ch, LANES), 1) < HD
            q = x_ref[Q, src, :]
            q0_s[dst, :] = jnp.where(first, q, 0.0).astype(BF16)
            q1_s[dst, :] = jnp.where(first, 0.0, q).astype(BF16)
            k_s[dst_pad, :] = x_ref[K, src, :].astype(BF16)
            v_s[dst_pad, :] = x_ref[V, src, :].astype(BF16)
            return carry

        @pl.when(phase == 2 * g)
        def _(ch=ch, deinterleave=deinterleave):
            lax.fori_loop(0, SEQ // ch, deinterleave, 0)

        def body(i, carry, g=g, dil=dil, n_blk=n_blk):
            blk = (phase % 2) * HALF_BLOCKS + i
            res = blk // n_blk
            pos = blk % n_blk
            rows = pl.ds(pl.multiple_of(blk * QB, QB), QB)
            rows2 = pl.ds(pl.multiple_of(blk * QB, QB), 2 * QB)
            k2 = k_s[rows2, :]
            v_aug = jnp.concatenate([v_s[rows2, :], ones], axis=1)
            bias = bias_ref[jnp.where(pos > 0, 0, 1)]
            q2 = jnp.concatenate([q0_s[rows, :], q1_s[rows, :]], axis=0)
            s = _dot_nt(q2, k2) + jnp.concatenate([bias, bias], axis=0)
            m = jnp.max(s, axis=1, keepdims=True)
            r = _dot(jnp.exp(s - m).astype(BF16), v_aug)
            r = jnp.where(jnp.concatenate([head0, head0], axis=1), r[:QB], r[QB:])
            l = r[:, LANES:]
            start = res + dil * QB * pos
            dst = pl.ds(pl.multiple_of(start, QB), QB) if dil == 1 else pl.ds(start, QB, stride=dil)
            o_refs[g][dst, :] = r[:, :LANES] / l
            lse_refs[g][dst, :] = jnp.where(head0, m[:QB], m[QB:]) + jnp.log(l)
            return carry

        @pl.when(phase // 2 == g)
        def _(body=body):
            lax.fori_loop(0, HALF_BLOCKS, body, 0, unroll=16)

    def finish(i, carry):
        blk = (phase % 2) * HALF_BLOCKS + i
        rows = pl.ds(pl.multiple_of(blk * QB, QB), QB)
        lses = [ref[rows, :] for ref in lse_refs]
        top = jnp.maximum(jnp.maximum(lses[0], lses[1]), lses[2])
        wts = [jnp.exp(lse - top) for lse in lses]
        num = wts[0] * o0_s[rows, :] + wts[1] * o1_s[rows, :] + wts[2] * o2_s[rows, :]
        den = wts[0] + wts[1] + wts[2]
        o_ref[rows, :] = (num / den * x_ref[Z, rows, :]).astype(o_ref.dtype)
        return carry

    @pl.when(phase // 2 == len(DILATION_GROUPS))
    def _():
        lax.fori_loop(0, HALF_BLOCKS, finish, 0)


def _sample_key_counts():
    i = (np.arange(HEADS_PER_STEP * DEC_SEQ) % DEC_SEQ)[:, None]

    def count(dist):
        total = np.zeros(dist.shape, np.float32)
        for window, dil in DILATION_GROUPS:
            total += (dist >= 0) & (dist <= window) & (dist % dil == 0)
        return total

    cache = count(CACHE_LEN + i - np.arange(CACHE_LEN)[None, :])
    new = count(i - np.arange(QB)[None, :]) * (np.arange(QB)[None, :] < DEC_SEQ)
    return cache, new.astype(np.float32)


def _sample_attention_step(x_ref, kt_ref, vt_ref, cc_ref, cn_ref, o_ref):
    width = HEADS_PER_STEP * HD
    n_q = HEADS_PER_STEP * DEC_SEQ
    wide = lambda kind: jnp.concatenate([x_ref[p, kind] for p in range(PAIRS_PER_STEP)], axis=1)
    lane = lax.broadcasted_iota(jnp.int32, (n_q, width), 1)
    row = lax.broadcasted_iota(jnp.int32, (n_q, width), 0)
    own_head = (lane // HD) == (row // DEC_SEQ)
    q_rep = jnp.concatenate([wide(Q)] * HEADS_PER_STEP, axis=0)
    q_big = jnp.where(own_head, q_rep, 0.0).astype(BF16)
    pad = jnp.zeros((QB - DEC_SEQ, width), F32)
    kn = jnp.concatenate([wide(K), pad], axis=0).astype(BF16)
    vn = jnp.concatenate([wide(V), pad], axis=0).astype(BF16)
    cc = cc_ref[...]
    cn = cn_ref[...]
    kt = kt_ref[...].reshape(width, CACHE_LEN).astype(BF16)
    s_c = jnp.where(cc > 0.0, _dot(q_big, kt), NEG)
    s_n = jnp.where(cn > 0.0, _dot_nt(q_big, kn), NEG)
    m = jnp.maximum(jnp.max(s_c, axis=1, keepdims=True), jnp.max(s_n, axis=1, keepdims=True))
    e_c = jnp.exp(s_c - m) * cc
    e_n = jnp.exp(s_n - m) * cn
    l = jnp.sum(e_c, axis=1, keepdims=True) + jnp.sum(e_n, axis=1, keepdims=True)
    vt = vt_ref[...].reshape(width, CACHE_LEN).astype(BF16)
    out_big = (_dot_nt(e_c.astype(BF16), vt) + _dot(e_n.astype(BF16), vn)) / l
    out_big = jnp.where(own_head, out_big, 0.0)
    out = jnp.sum(out_big.reshape(HEADS_PER_STEP, DEC_SEQ, width), axis=0)
    o_ref[...] = (out * wide(Z)).astype(o_ref.dtype)


def _attn_kernel(xs_ref, kt_ref, vt_ref, cc_ref, cn_ref, xp_ref, bias_ref, os_ref, op_ref, *scratch):
    _sample_attention_step(xs_ref, kt_ref, vt_ref, cc_ref, cn_ref, os_ref)
    _prompt_attention_step(pl.program_id(0) % STEPS_PER_UNIT, xp_ref, bias_ref, op_ref, *scratch)


def _attn(x, kt_all, vt_all, e, cc, cn, bias):
    first = ROWS_P // DEC_SEQ
    width = HEADS_PER_STEP * HD
    per_row = H_B // HEADS_PER_STEP
    n_steps = DEC_BATCH * per_row
    assert n_steps == BATCH * N_PAIRS * STEPS_PER_UNIT
    n_q = HEADS_PER_STEP * DEC_SEQ
    once = pl.Buffered(1)
    unit = lambda s: s // STEPS_PER_UNIT
    cache_spec = pl.BlockSpec((None, None, HEADS_PER_STEP, HD, CACHE_LEN),
                              lambda s: (e, s // per_row, s % per_row, 0, 0))
    return pl.pallas_call(
        _attn_kernel,
        grid=(n_steps,),
        in_specs=[pl.BlockSpec((PAIRS_PER_STEP, 4, DEC_SEQ, LANES), lambda s: (s % per_row, 0, first + s // per_row, 0)),
                  cache_spec, cache_spec,
                  pl.BlockSpec((n_q, CACHE_LEN), lambda s: (0, 0), pipeline_mode=once),
                  pl.BlockSpec((n_q, QB), lambda s: (0, 0), pipeline_mode=once),
                  pl.BlockSpec((None, 4, SEQ, LANES), lambda s: (unit(s) % N_PAIRS, 0, unit(s) // N_PAIRS, 0)),
                  pl.BlockSpec((2, QB, 2 * QB), lambda s: (0, 0, 0), pipeline_mode=once)],
        out_specs=[pl.BlockSpec((DEC_SEQ, width), lambda s: (s // per_row, s % per_row)),
                   pl.BlockSpec((SEQ, LANES), lambda s: (unit(s) // N_PAIRS, unit(s) % N_PAIRS))],
        out_shape=[jax.ShapeDtypeStruct((ROWS_S, W_B), BF16),
                   jax.ShapeDtypeStruct((ROWS_P, W_B), BF16)],
        scratch_shapes=[pltpu.VMEM((SEQ, LANES), BF16)] * 2
                       + [pltpu.VMEM((QB + SEQ, LANES), BF16)] * 2
                       + [pltpu.VMEM((SEQ, LANES), F32)] * 6,
        compiler_params=pltpu.CompilerParams(dimension_semantics=("arbitrary",),
                                             vmem_limit_bytes=ATTN_VMEM_LIMIT),
        name="attn",
    )(x, kt_all, vt_all, cc, cn, x, bias)


def _row_tiles(x, width, col=0):
    if not isinstance(x, tuple):
        return [pl.BlockSpec((TM, width), lambda i: (i, col))], [x]
    x_p, x_s = x
    return ([pl.BlockSpec((TM, width), lambda i: (jnp.minimum(i, N_TILES_P - 1), col)),
             pl.BlockSpec((TM, width), lambda i: (jnp.maximum(i - N_TILES_P, 0), col))],
            [x_p, x_s])


def _read_tile(refs):
    if len(refs) == 1:
        return refs[0][...]
    return jnp.where(pl.program_id(0) < N_TILES_P, refs[0][...], refs[1][...])


def _out_proj_kernel(n_b, n_h, a_ref, *refs):
    b_refs, refs = refs[:n_b], refs[n_b:]
    w_ref, refs = refs[0], refs[1:]
    h_refs, (g_ref, h_out_ref, n_out_ref) = refs[:n_h], refs[n_h:]
    half = w_ref.shape[0] // 2
    h_new = (_read_tile(h_refs) + _dot(a_ref[...], w_ref[0:half, :])
             + _dot(_read_tile(b_refs), w_ref[half:, :]))
    h_out_ref[...] = h_new
    ms = jnp.mean(h_new * h_new, axis=-1, keepdims=True)
    n_out_ref[...] = (h_new * lax.rsqrt(ms + EPS) * g_ref[...]).astype(n_out_ref.dtype)


def _out_proj(a, a_col, b, b_col, w, h, g, norm_dtype):
    half = w.shape[0] // 2
    b_specs, b_args = _row_tiles(b, half, b_col)
    h_specs, h_args = _row_tiles(h, D_MODEL)
    return pl.pallas_call(
        functools.partial(_out_proj_kernel, len(b_args), len(h_args)),
        grid=(N_TILES,),
        in_specs=[pl.BlockSpec((TM, half), lambda i: (i, a_col))] + b_specs
                 + [pl.BlockSpec((2 * half, D_MODEL), lambda i: (0, 0))] + h_specs
                 + [pl.BlockSpec((1, D_MODEL), lambda i: (0, 0))],
        out_specs=[pl.BlockSpec((TM, D_MODEL), lambda i: (i, 0)),
                   pl.BlockSpec((TM, D_MODEL), lambda i: (i, 0))],
        out_shape=[jax.ShapeDtypeStruct((ROWS, D_MODEL), F32),
                   jax.ShapeDtypeStruct((ROWS, D_MODEL), norm_dtype)],
        compiler_params=_params("arbitrary"),
        name="out_proj",
    )(a, *b_args, w, *h_args, g.reshape(1, D_MODEL))


TN_ODD = 512
N_COL_ODD = W_C // TN_ODD
PRE_SLOTS = BATCH + ROWS_S // TM


def _odd_in_kernel(x_ref, wb_ref, wc_ref, wx_ref, wz_ref, cw_ref, e1_ref, e2_ref, y_ref, pre_ref, buf):
    i = pl.program_id(1)

    @pl.when(i == 0)
    def _():
        buf[0:SUBLANES, :] = jnp.zeros((SUBLANES, TN_ODD), F32)

    x = x_ref[...]
    pre = _dot(x, wc_ref[...]) * _dot(x, wx_ref[...])
    pre_ref[...] = pre
    buf[SUBLANES:SUBLANES + TM, :] = pre
    row = lax.broadcasted_iota(jnp.int32, (TM, TN_ODD), 0)
    t = jnp.where(i < N_TILES_P, (i % TILES_PER_SEQ) * TM + row, row % DEC_SEQ)
    is_sample = i >= N_TILES_P
    pre_m1 = jnp.where(t >= 1, buf[SUBLANES - 1:SUBLANES - 1 + TM, :], jnp.where(is_sample, e1_ref[...], 0.0))
    pre_m2 = jnp.where(t >= 2, buf[SUBLANES - 2:SUBLANES - 2 + TM, :], jnp.where(is_sample, e2_ref[...], 0.0))
    conv = cw_ref[0:1, :] * pre_m2 + cw_ref[1:2, :] * pre_m1 + cw_ref[2:3, :] * pre
    y = _dot(x, wb_ref[...]) * conv * _silu(_dot(x, wz_ref[...]))
    y_ref[...] = y.astype(y_ref.dtype)
    buf[0:SUBLANES, :] = pre[TM - SUBLANES:, :]


def _odd_in(n, w, conv_w, e1, e2):
    wcol = lambda part: (lambda c, i: (0, part * N_COL_ODD + c))
    e_block = lambda c, i: (jnp.maximum(i - N_TILES_P, 0), c)
    pre_block = lambda c, i: (jnp.where(i < N_TILES_P, i // TILES_PER_SEQ, i - N_TILES_P + BATCH), c)
    return pl.pallas_call(
        _odd_in_kernel,
        grid=(N_COL_ODD, N_TILES),
        in_specs=[pl.BlockSpec((TM, D_MODEL), lambda c, i: (i, 0)),
                  pl.BlockSpec((D_MODEL, TN_ODD), wcol(0)),
                  pl.BlockSpec((D_MODEL, TN_ODD), wcol(1)),
                  pl.BlockSpec((D_MODEL, TN_ODD), wcol(2)),
                  pl.BlockSpec((D_MODEL, TN_ODD), wcol(3)),
                  pl.BlockSpec((CONV_W, TN_ODD), lambda c, i: (0, c)),
                  pl.BlockSpec((TM, TN_ODD), e_block),
                  pl.BlockSpec((TM, TN_ODD), e_block)],
        out_specs=[pl.BlockSpec((TM, TN_ODD), lambda c, i: (i, c)),
                   pl.BlockSpec((TM, TN_ODD), pre_block)],
        out_shape=[jax.ShapeDtypeStruct((ROWS, W_C), BF16),
                   jax.ShapeDtypeStruct((PRE_SLOTS * TM, W_C), F32)],
        scratch_shapes=[pltpu.VMEM((SUBLANES + TM, TN_ODD), F32)],
        compiler_params=_params("arbitrary", "arbitrary"),
        name="odd_in",
    )(n, w, w, w, w, conv_w, e1, e2)


def _rope_tables():
    half = HD // 2
    inv = 1.0 / (ROPE_THETA ** (jnp.arange(half, dtype=F32) / half))
    pos_s = PAST_LEN + (jnp.arange(TM_E) % DEC_SEQ)
    pos = jnp.concatenate([jnp.arange(SEQ), pos_s]).astype(F32)
    ang = pos[:, None] * inv[None, :]
    cos = jnp.cos(ang)
    sin = jnp.sin(ang)
    reps = LANES // HD
    return (jnp.tile(jnp.concatenate([cos, cos], axis=1), (1, reps)),
            jnp.tile(jnp.concatenate([-sin, sin], axis=1), (1, reps)))


def _gmlp_weights(w_s, b_s):
    tril = jnp.tril(jnp.ones((CHUNK, CHUNK), bool))
    w_p = jnp.where(tril[None], w_s, 0.0)
    small = w_p[:, :DEC_SEQ, :DEC_SEQ]
    eye = jnp.eye(CHUNK // DEC_SEQ, dtype=F32)
    w_smp = jnp.einsum('ab,hts->hatbs', eye, small).reshape(H_A, CHUNK, CHUNK)
    wm = jnp.stack([w_p, w_smp]).astype(BF16)
    bias_p = jnp.repeat(b_s.T, CH_A, axis=1)
    bias_s = jnp.tile(bias_p[:DEC_SEQ], (CHUNK // DEC_SEQ, 1))
    return wm, jnp.stack([bias_p, bias_s])


def _conv_edges(state):
    e1 = jnp.pad(state[:, CONV_W - 2:], ((0, 0), (0, DEC_SEQ - 1), (0, 0)))
    e2 = jnp.pad(state, ((0, 0), (0, DEC_SEQ - (CONV_W - 1)), (0, 0)))
    return e1.reshape(ROWS_S, W_C), e2.reshape(ROWS_S, W_C)


def _new_kv(x, buf_p):
    kv_p = lax.slice(x, (0, K, 0, 0), (N_PAIRS, V + 1, ROWS_P, LANES)).reshape(N_PAIRS, 2, BATCH, SEQ, LANES)
    kv_p = jnp.transpose(kv_p[:, :, :, SEQ - buf_p:], (1, 2, 3, 0, 4)).reshape(2, BATCH, buf_p, H_B, HD)
    kv_s = lax.slice(x, (0, K, ROWS_P, 0), (N_PAIRS, V + 1, ROWS, LANES))
    kv_s = jnp.transpose(kv_s, (1, 2, 0, 3)).reshape(2, DEC_BATCH, DEC_SEQ, H_B, HD)
    return kv_p, kv_s


def kernel(x_prompt, x_sample, cache_b_k, cache_b_v, state_c_conv, norm_w, final_norm_w,
           w_in_even, w_s, b_s, w_out_even, w_in_odd, conv_w, w_out_odd):
    h = (x_prompt.reshape(ROWS_P, D_MODEL), x_sample.reshape(ROWS_S, D_MODEL))
    cos, sin = _rope_tables()
    band_bias = jnp.asarray(_band_bias())
    cc, cn = (jnp.asarray(c) for c in _sample_key_counts())
    kt_all = jnp.transpose(cache_b_k, (0, 1, 3, 4, 2))
    vt_all = jnp.transpose(cache_b_v, (0, 1, 3, 4, 2))
    n = _rmsnorm(h, norm_w[0])
    kp_l, vp_l, ks_l, vs_l, av_l, cp_l, cs_l = [], [], [], [], [], [], []
    buf_p = min(W_MAX, SEQ)
    for layer in range(DEPTH):
        last = layer == DEPTH - 1
        g_next = final_norm_w if last else norm_w[layer + 1]
        norm_dtype = F32 if last else BF16
        if layer % 2 == 0:
            e = layer // 2
            wm, bias = _gmlp_weights(w_s[e], b_s[e])
            a_out, av, att_in = _even_in(n, w_in_even[e].astype(BF16), cos, sin, wm, bias)
            att_s, att_p = _attn(att_in, kt_all, vt_all, e, cc, cn, band_bias)
            h, n = _out_proj(a_out, 0, (att_p, att_s), 0, w_out_even[e].astype(BF16), h, g_next, norm_dtype)
            kv_p, kv_s = _new_kv(att_in, buf_p)
            kp_l.append(kv_p[0])
            vp_l.append(kv_p[1])
            ks_l.append(kv_s[0])
            vs_l.append(kv_s[1])
            av_l.append(av)
        else:
            c = layer // 2
            e1, e2 = _conv_edges(state_c_conv[c])
            y, pre = _odd_in(n, w_in_odd[c].astype(BF16), conv_w[c], e1, e2)
            h, n = _out_proj(y, 0, y, 1, w_out_odd[c].astype(BF16), h, g_next, norm_dtype)
            pre_p = pre[:BATCH * TM].reshape(BATCH, TM, W_C)
            cp_l.append(pre_p[:, TM - (CONV_W - 1):])
            cs_l.append(pre[BATCH * TM:].reshape(DEC_BATCH, DEC_SEQ, W_C)[:, DEC_SEQ - (CONV_W - 1):])
    y_prompt = n[:ROWS_P].reshape(BATCH, SEQ, D_MODEL)
    y_sample = n[ROWS_P:].reshape(DEC_BATCH, DEC_SEQ, D_MODEL)
    new_a_v_sample = jnp.stack(av_l).reshape(N_EVEN, DEC_BATCH, DEC_SEQ, W_A)
    return (y_prompt, y_sample, jnp.stack(kp_l), jnp.stack(vp_l), jnp.stack(ks_l), jnp.stack(vs_l),
            new_a_v_sample, jnp.stack(cp_l), jnp.stack(cs_l))
```

```python
import functools

import numpy as np
import jax
import jax.numpy as jnp
from jax import lax
from jax.experimental import pallas as pl
from jax.experimental.pallas import tpu as pltpu

D_MODEL = 1024
BATCH = 4
SEQ = 4096
DEPTH = 4
DEC_BATCH = 128
DEC_SEQ = 8
PAST_LEN = 2048
N_EVEN = (DEPTH + 1) // 2
N_ODD = DEPTH // 2
W_A = D_MODEL
CHUNK = 128
H_A = 8
CH_A = W_A // H_A
W_B = D_MODEL
H_B = 16
HD = W_B // H_B
DILATION_GROUPS = ((128, 1), (512, 4), (2048, 16))
W_MAX = 2048
ROPE_THETA = 10000.0
W_C = 2 * D_MODEL
CONV_W = 3
EPS = 1e-6
NEG = -1e30
EVEN_IN = 3 * W_A + 4 * W_B
ODD_IN = 4 * W_C

LANES = 128
SUBLANES = 8
ROWS_P = BATCH * SEQ
ROWS_S = DEC_BATCH * DEC_SEQ
ROWS = ROWS_P + ROWS_S
TM = 512
N_TILES = ROWS // TM
N_TILES_P = ROWS_P // TM
TILES_PER_SEQ = SEQ // TM
QB = 128
CACHE_LEN = min(W_MAX, PAST_LEN)
HEADS_PER_STEP = 8
N_PAIRS = W_B // LANES
PAIRS_PER_STEP = HEADS_PER_STEP * HD // LANES
Q, K, V, Z = range(4)
VMEM_LIMIT = 48 * 1024 * 1024
ATTN_VMEM_LIMIT = 60 * 1024 * 1024

assert all(window // dil == QB for window, dil in DILATION_GROUPS)

F32 = jnp.float32
BF16 = jnp.bfloat16


def _dot(a, b):
    return jnp.dot(a, b, preferred_element_type=F32)


def _dot_nt(a, b):
    return lax.dot_general(a, b, (((1,), (1,)), ((), ())), preferred_element_type=F32)


def _params(*sem):
    return pltpu.CompilerParams(dimension_semantics=sem, vmem_limit_bytes=VMEM_LIMIT)


def _rmsnorm_kernel(*refs):
    x_refs, (g_ref, o_ref) = refs[:-2], refs[-2:]
    x = _read_tile(x_refs)
    ms = jnp.mean(x * x, axis=-1, keepdims=True)
    o_ref[...] = (x * lax.rsqrt(ms + EPS) * g_ref[...]).astype(o_ref.dtype)


def _rmsnorm(x, g):
    x_specs, x_args = _row_tiles(x, D_MODEL)
    return pl.pallas_call(
        _rmsnorm_kernel,
        grid=(N_TILES,),
        in_specs=x_specs + [pl.BlockSpec((1, D_MODEL), lambda i: (0, 0))],
        out_specs=pl.BlockSpec((TM, D_MODEL), lambda i: (i, 0)),
        out_shape=jax.ShapeDtypeStruct((ROWS, D_MODEL), BF16),
        compiler_params=_params("arbitrary"),
        name="rmsnorm",
    )(*x_args, g.reshape(1, D_MODEL))


def _gelu(x):
    return 0.5 * x * (1.0 + lax.erf(x * np.float32(np.sqrt(0.5))))


def _silu(x):
    return x * jax.nn.sigmoid(x)


TM_E = 256
N_TILES_E = ROWS // TM_E
N_TILES_E_P = ROWS_P // TM_E


def _rope(x, cos, sin, o_ref, kv_ref, kind, scale):
    lane = lax.broadcasted_iota(jnp.int32, (TM_E, LANES), 1)
    first_half = (lane % HD) < (HD // 2)
    for p in range(N_PAIRS):
        xc = x[:, p * LANES:(p + 1) * LANES]
        partner = jnp.where(first_half, pltpu.roll(xc, LANES - HD // 2, 1), pltpu.roll(xc, HD // 2, 1))
        out = xc * cos + partner * sin
        if scale is not None:
            out = out * scale
        o_ref[p, kind] = out
        if kind == K:
            kv_ref[:, p * LANES:(p + 1) * LANES] = out


def _even_in_kernel(x_ref, w_ref, cos_ref, sin_ref, wm_ref, bias_ref, a_ref, av_ref, o_ref, kv_ref):
    x = x_ref[...]
    col = lambda j: w_ref[:, j * D_MODEL:(j + 1) * D_MODEL]
    v = _gelu(_dot(x, col(1)))
    av_ref[...] = v
    v_bf = v.astype(BF16)
    u = _gelu(_dot(x, col(0)))
    z = _silu(_dot(x, col(2)))
    for c in range(TM_E // CHUNK):
        rows = slice(c * CHUNK, (c + 1) * CHUNK)
        for h in range(H_A):
            sl = slice(h * CH_A, (h + 1) * CH_A)
            mix = _dot(wm_ref[h], v_bf[rows, sl]) + bias_ref[:, sl]
            a_ref[rows, sl] = (u[rows, sl] * mix * z[rows, sl]).astype(a_ref.dtype)
    cos = cos_ref[...]
    sin = sin_ref[...]
    _rope(_dot(x, col(3)), cos, sin, o_ref, kv_ref, Q, np.float32(HD ** -0.5))
    _rope(_dot(x, col(4)), cos, sin, o_ref, kv_ref, K, None)
    v_b = _dot(x, col(5))
    kv_ref[:, W_B:2 * W_B] = v_b
    z_b = _silu(_dot(x, col(6)))
    for p in range(N_PAIRS):
        o_ref[p, V] = v_b[:, p * LANES:(p + 1) * LANES]
        o_ref[p, Z] = z_b[:, p * LANES:(p + 1) * LANES]


def _even_in(n, w, cos, sin, wm, bias):
    tiles_per_seq = SEQ // TM_E
    kept = min(W_MAX, SEQ) // TM_E
    table = lambda i: (jnp.where(i < N_TILES_E_P, i % tiles_per_seq, tiles_per_seq), 0)
    variant = lambda i: jnp.where(i < N_TILES_E_P, 0, 1)
    kv_block = lambda i: (jnp.where(i < N_TILES_E_P,
                                    i // tiles_per_seq * kept + jnp.maximum(i % tiles_per_seq - (tiles_per_seq - kept), 0),
                                    BATCH * kept + i - N_TILES_E_P), 0)
    return pl.pallas_call(
        _even_in_kernel,
        grid=(N_TILES_E,),
        in_specs=[pl.BlockSpec((TM_E, D_MODEL), lambda i: (i, 0)),
                  pl.BlockSpec((D_MODEL, EVEN_IN), lambda i: (0, 0), pipeline_mode=pl.Buffered(1)),
                  pl.BlockSpec((TM_E, LANES), table),
                  pl.BlockSpec((TM_E, LANES), table),
                  pl.BlockSpec((None, H_A, CHUNK, CHUNK), lambda i: (variant(i), 0, 0, 0)),
                  pl.BlockSpec((None, CHUNK, W_A), lambda i: (variant(i), 0, 0))],
        out_specs=[pl.BlockSpec((TM_E, W_A), lambda i: (i, 0)),
                   pl.BlockSpec((TM_E, W_A), lambda i: (jnp.maximum(i - N_TILES_E_P, 0), 0)),
                   pl.BlockSpec((N_PAIRS, 4, TM_E, LANES), lambda i: (0, 0, i, 0)),
                   pl.BlockSpec((TM_E, 2 * W_B), kv_block)],
        out_shape=[jax.ShapeDtypeStruct((ROWS, W_A), BF16),
                   jax.ShapeDtypeStruct((ROWS_S, W_A), F32),
                   jax.ShapeDtypeStruct((N_PAIRS, 4, ROWS, LANES), F32),
                   jax.ShapeDtypeStruct((BATCH * kept * TM_E + ROWS_S, 2 * W_B), F32)],
        compiler_params=_params("arbitrary"),
        name="even_in",
    )(n, w, cos, sin, wm, bias)


def _band_bias():
    i = np.arange(QB)[:, None]
    j = np.arange(2 * QB)[None, :]
    band = (j - i >= 0) & (j - i <= QB)
    first = band & (j >= QB)
    return np.where(np.stack([band, first]), 0.0, NEG).astype(np.float32)


STEPS_PER_UNIT = 8
HALF_BLOCKS = SEQ // QB // 2


def _prompt_attention_step(phase, x_ref, bias_ref, o_ref,
                           q0_s, q1_s, k_s, v_s, o0_s, o1_s, o2_s, lse0_s, lse1_s, lse2_s):
    o_refs = (o0_s, o1_s, o2_s)
    lse_refs = (lse0_s, lse1_s, lse2_s)
    head0 = lax.broadcasted_iota(jnp.int32, (QB, LANES), 1) < HD
    ones = jnp.ones((2 * QB, LANES), BF16)

    @pl.when(phase == 0)
    def _():
        k_s[0:QB, :] = jnp.zeros((QB, LANES), BF16)
        v_s[0:QB, :] = jnp.zeros((QB, LANES), BF16)

    for g, (_, dil) in enumerate(DILATION_GROUPS):
        n = SEQ // dil
        ch = min(n, 4 * QB)
        n_ch = n // ch
        n_blk = n // QB

        def deinterleave(idx, carry, dil=dil, ch=ch, n_ch=n_ch):
            start = idx // n_ch + dil * ch * (idx % n_ch)
            src = pl.ds(pl.multiple_of(start, ch), ch) if dil == 1 else pl.ds(start, ch, stride=dil)
            dst = pl.ds(pl.multiple_of(idx * ch, QB), ch)
            dst_pad = pl.ds(pl.multiple_of(QB + idx * ch, QB), ch)
            first = lax.broadcasted_iota(jnp.int32, (ch, LANES), 1) < HD
            q = x_ref[Q, src, :]
            q0_s[dst, :] = jnp.where(first, q, 0.0).astype(BF16)
            q1_s[dst, :] = jnp.where(first, 0.0, q).astype(BF16)
            k_s[dst_pad, :] = x_ref[K, src, :].astype(BF16)
            v_s[dst_pad, :] = x_ref[V, src, :].astype(BF16)
            return carry

        @pl.when(phase == 2 * g)
        def _(ch=ch, deinterleave=deinterleave):
            lax.fori_loop(0, SEQ // ch, deinterleave, 0)

        def body(i, carry, g=g, dil=dil, n_blk=n_blk):
            blk = (phase % 2) * HALF_BLOCKS + i
            res = blk // n_blk
            pos = blk % n_blk
            rows = pl.ds(pl.multiple_of(blk * QB, QB), QB)
            rows2 = pl.ds(pl.multiple_of(blk * QB, QB), 2 * QB)
            k2 = k_s[rows2, :]
            v_aug = jnp.concatenate([v_s[rows2, :], ones], axis=1)
            bias = bias_ref[jnp.where(pos > 0, 0, 1)]
            q2 = jnp.concatenate([q0_s[rows, :], q1_s[rows, :]], axis=0)
            s = _dot_nt(q2, k2) + jnp.concatenate([bias, bias], axis=0)
            m = jnp.max(s, axis=1, keepdims=True)
            r = _dot(jnp.exp(s - m).astype(BF16), v_aug)
            r = jnp.where(jnp.concatenate([head0, head0], axis=1), r[:QB], r[QB:])
            l = r[:, LANES:]
            start = res + dil * QB * pos
            dst = pl.ds(pl.multiple_of(start, QB), QB) if dil == 1 else pl.ds(start, QB, stride=dil)
            o_refs[g][dst, :] = r[:, :LANES] / l
            lse_refs[g][dst, :] = jnp.where(head0, m[:QB], m[QB:]) + jnp.log(l)
            return carry

        @pl.when(phase // 2 == g)
        def _(body=body):
            lax.fori_loop(0, HALF_BLOCKS, body, 0, unroll=16)

    def finish(i, carry):
        blk = (phase % 2) * HALF_BLOCKS + i
        rows = pl.ds(pl.multiple_of(blk * QB, QB), QB)
        lses = [ref[rows, :] for ref in lse_refs]
        top = jnp.maximum(jnp.maximum(lses[0], lses[1]), lses[2])
        wts = [jnp.exp(lse - top) for lse in lses]
        num = wts[0] * o0_s[rows, :] + wts[1] * o1_s[rows, :] + wts[2] * o2_s[rows, :]
        den = wts[0] + wts[1] + wts[2]
        o_ref[rows, :] = (num / den * x_ref[Z, rows, :]).astype(o_ref.dtype)
        return carry

    @pl.when(phase // 2 == len(DILATION_GROUPS))
    def _():
        lax.fori_loop(0, HALF_BLOCKS, finish, 0)


def _sample_key_counts():
    i = (np.arange(HEADS_PER_STEP * DEC_SEQ) % DEC_SEQ)[:, None]

    def count(dist):
        total = np.zeros(dist.shape, np.float32)
        for window, dil in DILATION_GROUPS:
            total += (dist >= 0) & (dist <= window) & (dist % dil == 0)
        return total

    cache = count(CACHE_LEN + i - np.arange(CACHE_LEN)[None, :])
    new = count(i - np.arange(QB)[None, :]) * (np.arange(QB)[None, :] < DEC_SEQ)
    return cache, new.astype(np.float32)


def _sample_attention_step(x_ref, kt_ref, vt_ref, cc_ref, cn_ref, o_ref):
    width = HEADS_PER_STEP * HD
    n_q = HEADS_PER_STEP * DEC_SEQ
    wide = lambda kind: jnp.concatenate([x_ref[p, kind] for p in range(PAIRS_PER_STEP)], axis=1)
    lane = lax.broadcasted_iota(jnp.int32, (n_q, width), 1)
    row = lax.broadcasted_iota(jnp.int32, (n_q, width), 0)
    own_head = (lane // HD) == (row // DEC_SEQ)
    q_rep = jnp.concatenate([wide(Q)] * HEADS_PER_STEP, axis=0)
    q_big = jnp.where(own_head, q_rep, 0.0).astype(BF16)
    pad = jnp.zeros((QB - DEC_SEQ, width), F32)
    kn = jnp.concatenate([wide(K), pad], axis=0).astype(BF16)
    vn = jnp.concatenate([wide(V), pad], axis=0).astype(BF16)
    cc = cc_ref[...]
    cn = cn_ref[...]
    kt = kt_ref[...].reshape(width, CACHE_LEN).astype(BF16)
    s_c = jnp.where(cc > 0.0, _dot(q_big, kt), NEG)
    s_n = jnp.where(cn > 0.0, _dot_nt(q_big, kn), NEG)
    m = jnp.maximum(jnp.max(s_c, axis=1, keepdims=True), jnp.max(s_n, axis=1, keepdims=True))
    e_c = jnp.exp(s_c - m) * cc
    e_n = jnp.exp(s_n - m) * cn
    l = jnp.sum(e_c, axis=1, keepdims=True) + jnp.sum(e_n, axis=1, keepdims=True)
    vt = vt_ref[...].reshape(width, CACHE_LEN).astype(BF16)
    out_big = (_dot_nt(e_c.astype(BF16), vt) + _dot(e_n.astype(BF16), vn)) / l
    out_big = jnp.where(own_head, out_big, 0.0)
    out = jnp.sum(out_big.reshape(HEADS_PER_STEP, DEC_SEQ, width), axis=0)
    o_ref[...] = (out * wide(Z)).astype(o_ref.dtype)


def _attn_kernel(xs_ref, kt_ref, vt_ref, cc_ref, cn_ref, xp_ref, bias_ref, os_ref, op_ref, *scratch):
    _sample_attention_step(xs_ref, kt_ref, vt_ref, cc_ref, cn_ref, os_ref)
    _prompt_attention_step(pl.program_id(0) % STEPS_PER_UNIT, xp_ref, bias_ref, op_ref, *scratch)


def _attn(x, kt_all, vt_all, e, cc, cn, bias):
    first = ROWS_P // DEC_SEQ
    width = HEADS_PER_STEP * HD
    per_row = H_B // HEADS_PER_STEP
    n_steps = DEC_BATCH * per_row
    assert n_steps == BATCH * N_PAIRS * STEPS_PER_UNIT
    n_q = HEADS_PER_STEP * DEC_SEQ
    once = pl.Buffered(1)
    unit = lambda s: s // STEPS_PER_UNIT
    cache_spec = pl.BlockSpec((None, None, HEADS_PER_STEP, HD, CACHE_LEN),
                              lambda s: (e, s // per_row, s % per_row, 0, 0))
    return pl.pallas_call(
        _attn_kernel,
        grid=(n_steps,),
        in_specs=[pl.BlockSpec((PAIRS_PER_STEP, 4, DEC_SEQ, LANES), lambda s: (s % per_row, 0, first + s // per_row, 0)),
                  cache_spec, cache_spec,
                  pl.BlockSpec((n_q, CACHE_LEN), lambda s: (0, 0), pipeline_mode=once),
                  pl.BlockSpec((n_q, QB), lambda s: (0, 0), pipeline_mode=once),
                  pl.BlockSpec((None, 4, SEQ, LANES), lambda s: (unit(s) % N_PAIRS, 0, unit(s) // N_PAIRS, 0)),
                  pl.BlockSpec((2, QB, 2 * QB), lambda s: (0, 0, 0), pipeline_mode=once)],
        out_specs=[pl.BlockSpec((DEC_SEQ, width), lambda s: (s // per_row, s % per_row)),
                   pl.BlockSpec((SEQ, LANES), lambda s: (unit(s) // N_PAIRS, unit(s) % N_PAIRS))],
        out_shape=[jax.ShapeDtypeStruct((ROWS_S, W_B), BF16),
                   jax.ShapeDtypeStruct((ROWS_P, W_B), BF16)],
        scratch_shapes=[pltpu.VMEM((SEQ, LANES), BF16)] * 2
                       + [pltpu.VMEM((QB + SEQ, LANES), BF16)] * 2
                       + [pltpu.VMEM((SEQ, LANES), F32)] * 6,
        compiler_params=pltpu.CompilerParams(dimension_semantics=("arbitrary",),
                                             vmem_limit_bytes=ATTN_VMEM_LIMIT),
        name="attn",
    )(x, kt_all, vt_all, cc, cn, x, bias)


def _row_tiles(x, width, col=0):
    if not isinstance(x, tuple):
        return [pl.BlockSpec((TM, width), lambda i: (i, col))], [x]
    x_p, x_s = x
    return ([pl.BlockSpec((TM, width), lambda i: (jnp.minimum(i, N_TILES_P - 1), col)),
             pl.BlockSpec((TM, width), lambda i: (jnp.maximum(i - N_TILES_P, 0), col))],
            [x_p, x_s])


def _read_tile(refs):
    if len(refs) == 1:
        return refs[0][...]
    return jnp.where(pl.program_id(0) < N_TILES_P, refs[0][...], refs[1][...])


def _out_proj_kernel(n_b, n_h, a_ref, *refs):
    b_refs, refs = refs[:n_b], refs[n_b:]
    w_ref, refs = refs[0], refs[1:]
    h_refs, (g_ref, h_out_ref, n_out_ref) = refs[:n_h], refs[n_h:]
    half = w_ref.shape[0] // 2
    h_new = (_read_tile(h_refs) + _dot(a_ref[...], w_ref[0:half, :])
             + _dot(_read_tile(b_refs), w_ref[half:, :]))
    h_out_ref[...] = h_new
    ms = jnp.mean(h_new * h_new, axis=-1, keepdims=True)
    n_out_ref[...] = (h_new * lax.rsqrt(ms + EPS) * g_ref[...]).astype(n_out_ref.dtype)


def _out_proj(a, a_col, b, b_col, w, h, g, norm_dtype):
    half = w.shape[0] // 2
    b_specs, b_args = _row_tiles(b, half, b_col)
    h_specs, h_args = _row_tiles(h, D_MODEL)
    return pl.pallas_call(
        functools.partial(_out_proj_kernel, len(b_args), len(h_args)),
        grid=(N_TILES,),
        in_specs=[pl.BlockSpec((TM, half), lambda i: (i, a_col))] + b_specs
                 + [pl.BlockSpec((2 * half, D_MODEL), lambda i: (0, 0))] + h_specs
                 + [pl.BlockSpec((1, D_MODEL), lambda i: (0, 0))],
        out_specs=[pl.BlockSpec((TM, D_MODEL), lambda i: (i, 0)),
                   pl.BlockSpec((TM, D_MODEL), lambda i: (i, 0))],
        out_shape=[jax.ShapeDtypeStruct((ROWS, D_MODEL), F32),
                   jax.ShapeDtypeStruct((ROWS, D_MODEL), norm_dtype)],
        compiler_params=_params("arbitrary"),
        name="out_proj",
    )(a, *b_args, w, *h_args, g.reshape(1, D_MODEL))


TN_ODD = 512
N_COL_ODD = W_C // TN_ODD
PRE_SLOTS = BATCH + ROWS_S // TM


def _odd_in_kernel(x_ref, wb_ref, wc_ref, wx_ref, wz_ref, cw_ref, e1_ref, e2_ref, y_ref, pre_ref, buf):
    i = pl.program_id(1)

    @pl.when(i == 0)
    def _():
        buf[0:SUBLANES, :] = jnp.zeros((SUBLANES, TN_ODD), F32)

    x = x_ref[...]
    pre = _dot(x, wc_ref[...]) * _dot(x, wx_ref[...])
    pre_ref[...] = pre
    buf[SUBLANES:SUBLANES + TM, :] = pre
    row = lax.broadcasted_iota(jnp.int32, (TM, TN_ODD), 0)
    t = jnp.where(i < N_TILES_P, (i % TILES_PER_SEQ) * TM + row, row % DEC_SEQ)
    is_sample = i >= N_TILES_P
    pre_m1 = jnp.where(t >= 1, buf[SUBLANES - 1:SUBLANES - 1 + TM, :], jnp.where(is_sample, e1_ref[...], 0.0))
    pre_m2 = jnp.where(t >= 2, buf[SUBLANES - 2:SUBLANES - 2 + TM, :], jnp.where(is_sample, e2_ref[...], 0.0))
    conv = cw_ref[0:1, :] * pre_m2 + cw_ref[1:2, :] * pre_m1 + cw_ref[2:3, :] * pre
    y = _dot(x, wb_ref[...]) * conv * _silu(_dot(x, wz_ref[...]))
    y_ref[...] = y.astype(y_ref.dtype)
    buf[0:SUBLANES, :] = pre[TM - SUBLANES:, :]


def _odd_in(n, w, conv_w, e1, e2):
    wcol = lambda part: (lambda c, i: (0, part * N_COL_ODD + c))
    e_block = lambda c, i: (jnp.maximum(i - N_TILES_P, 0), c)
    pre_block = lambda c, i: (jnp.where(i < N_TILES_P, i // TILES_PER_SEQ, i - N_TILES_P + BATCH), c)
    return pl.pallas_call(
        _odd_in_kernel,
        grid=(N_COL_ODD, N_TILES),
        in_specs=[pl.BlockSpec((TM, D_MODEL), lambda c, i: (i, 0)),
                  pl.BlockSpec((D_MODEL, TN_ODD), wcol(0)),
                  pl.BlockSpec((D_MODEL, TN_ODD), wcol(1)),
                  pl.BlockSpec((D_MODEL, TN_ODD), wcol(2)),
                  pl.BlockSpec((D_MODEL, TN_ODD), wcol(3)),
                  pl.BlockSpec((CONV_W, TN_ODD), lambda c, i: (0, c)),
                  pl.BlockSpec((TM, TN_ODD), e_block),
                  pl.BlockSpec((TM, TN_ODD), e_block)],
        out_specs=[pl.BlockSpec((TM, TN_ODD), lambda c, i: (i, c)),
                   pl.BlockSpec((TM, TN_ODD), pre_block)],
        out_shape=[jax.ShapeDtypeStruct((ROWS, W_C), BF16),
                   jax.ShapeDtypeStruct((PRE_SLOTS * TM, W_C), F32)],
        scratch_shapes=[pltpu.VMEM((SUBLANES + TM, TN_ODD), F32)],
        compiler_params=_params("arbitrary", "arbitrary"),
        name="odd_in",
    )(n, w, w, w, w, conv_w, e1, e2)


def _rope_tables():
    half = HD // 2
    inv = 1.0 / (ROPE_THETA ** (jnp.arange(half, dtype=F32) / half))
    pos_s = PAST_LEN + (jnp.arange(TM_E) % DEC_SEQ)
    pos = jnp.concatenate([jnp.arange(SEQ), pos_s]).astype(F32)
    ang = pos[:, None] * inv[None, :]
    cos = jnp.cos(ang)
    sin = jnp.sin(ang)
    reps = LANES // HD
    return (jnp.tile(jnp.concatenate([cos, cos], axis=1), (1, reps)),
            jnp.tile(jnp.concatenate([-sin, sin], axis=1), (1, reps)))


def _gmlp_weights(w_s, b_s):
    tril = jnp.tril(jnp.ones((CHUNK, CHUNK), bool))
    w_p = jnp.where(tril[None], w_s, 0.0)
    small = w_p[:, :DEC_SEQ, :DEC_SEQ]
    eye = jnp.eye(CHUNK // DEC_SEQ, dtype=F32)
    w_smp = jnp.einsum('ab,hts->hatbs', eye, small).reshape(H_A, CHUNK, CHUNK)
    wm = jnp.stack([w_p, w_smp]).astype(BF16)
    bias_p = jnp.repeat(b_s.T, CH_A, axis=1)
    bias_s = jnp.tile(bias_p[:DEC_SEQ], (CHUNK // DEC_SEQ, 1))
    return wm, jnp.stack([bias_p, bias_s])


def _conv_edges(state):
    e1 = jnp.pad(state[:, CONV_W - 2:], ((0, 0), (0, DEC_SEQ - 1), (0, 0)))
    e2 = jnp.pad(state, ((0, 0), (0, DEC_SEQ - (CONV_W - 1)), (0, 0)))
    return e1.reshape(ROWS_S, W_C), e2.reshape(ROWS_S, W_C)


def _new_kv(kv, buf_p):
    n_p = BATCH * buf_p
    split = lambda rows, shape: [lax.slice(kv, (rows[0], c * W_B), (rows[1], (c + 1) * W_B)).reshape(shape)
                                for c in range(2)]
    return (split((0, n_p), (BATCH, buf_p, H_B, HD)),
            split((n_p, n_p + ROWS_S), (DEC_BATCH, DEC_SEQ, H_B, HD)))


def kernel(x_prompt, x_sample, cache_b_k, cache_b_v, state_c_conv, norm_w, final_norm_w,
           w_in_even, w_s, b_s, w_out_even, w_in_odd, conv_w, w_out_odd):
    h = (x_prompt.reshape(ROWS_P, D_MODEL), x_sample.reshape(ROWS_S, D_MODEL))
    cos, sin = _rope_tables()
    band_bias = jnp.asarray(_band_bias())
    cc, cn = (jnp.asarray(c) for c in _sample_key_counts())
    kt_all = jnp.transpose(cache_b_k, (0, 1, 3, 4, 2))
    vt_all = jnp.transpose(cache_b_v, (0, 1, 3, 4, 2))
    n = _rmsnorm(h, norm_w[0])
    kp_l, vp_l, ks_l, vs_l, av_l, cp_l, cs_l = [], [], [], [], [], [], []
    buf_p = min(W_MAX, SEQ)
    for layer in range(DEPTH):
        last = layer == DEPTH - 1
        g_next = final_norm_w if last else norm_w[layer + 1]
        norm_dtype = F32 if last else BF16
        if layer % 2 == 0:
            e = layer // 2
            wm, bias = _gmlp_weights(w_s[e], b_s[e])
            a_out, av, att_in, kv = _even_in(n, w_in_even[e].astype(BF16), cos, sin, wm, bias)
            att_s, att_p = _attn(att_in, kt_all, vt_all, e, cc, cn, band_bias)
            h, n = _out_proj(a_out, 0, (att_p, att_s), 0, w_out_even[e].astype(BF16), h, g_next, norm_dtype)
            kv_p, kv_s = _new_kv(kv, buf_p)
            kp_l.append(kv_p[0])
            vp_l.append(kv_p[1])
            ks_l.append(kv_s[0])
            vs_l.append(kv_s[1])
            av_l.append(av)
        else:
            c = layer // 2
            e1, e2 = _conv_edges(state_c_conv[c])
            y, pre = _odd_in(n, w_in_odd[c].astype(BF16), conv_w[c], e1, e2)
            h, n = _out_proj(y, 0, y, 1, w_out_odd[c].astype(BF16), h, g_next, norm_dtype)
            pre_p = pre[:BATCH * TM].reshape(BATCH, TM, W_C)
            cp_l.append(pre_p[:, TM - (CONV_W - 1):])
            cs_l.append(pre[BATCH * TM:].reshape(DEC_BATCH, DEC_SEQ, W_C)[:, DEC_SEQ - (CONV_W - 1):])
    y_prompt = n[:ROWS_P].reshape(BATCH, SEQ, D_MODEL)
    y_sample = n[ROWS_P:].reshape(DEC_BATCH, DEC_SEQ, D_MODEL)
    new_a_v_sample = jnp.stack(av_l).reshape(N_EVEN, DEC_BATCH, DEC_SEQ, W_A)
    return (y_prompt, y_sample, jnp.stack(kp_l), jnp.stack(vp_l), jnp.stack(ks_l), jnp.stack(vs_l),
            new_a_v_sample, jnp.stack(cp_l), jnp.stack(cs_l))
```

```python
import functools

import numpy as np
import jax
import jax.numpy as jnp
from jax import lax
from jax.experimental import pallas as pl
from jax.experimental.pallas import tpu as pltpu

D_MODEL = 1024
BATCH = 4
SEQ = 4096
DEPTH = 4
DEC_BATCH = 128
DEC_SEQ = 8
PAST_LEN = 2048
N_EVEN = (DEPTH + 1) // 2
N_ODD = DEPTH // 2
W_A = D_MODEL
CHUNK = 128
H_A = 8
CH_A = W_A // H_A
W_B = D_MODEL
H_B = 16
HD = W_B // H_B
DILATION_GROUPS = ((128, 1), (512, 4), (2048, 16))
W_MAX = 2048
ROPE_THETA = 10000.0
W_C = 2 * D_MODEL
CONV_W = 3
EPS = 1e-6
NEG = -1e30
EVEN_IN = 3 * W_A + 4 * W_B
ODD_IN = 4 * W_C

LANES = 128
SUBLANES = 8
ROWS_P = BATCH * SEQ
ROWS_S = DEC_BATCH * DEC_SEQ
ROWS = ROWS_P + ROWS_S
TM = 512
N_TILES = ROWS // TM
N_TILES_P = ROWS_P // TM
TILES_PER_SEQ = SEQ // TM
QB = 128
CACHE_LEN = min(W_MAX, PAST_LEN)
HEADS_PER_STEP = 8
N_PAIRS = W_B // LANES
PAIRS_PER_STEP = HEADS_PER_STEP * HD // LANES
Q, K, V, Z = range(4)
VMEM_LIMIT = 48 * 1024 * 1024
ATTN_VMEM_LIMIT = 60 * 1024 * 1024

assert all(window // dil == QB for window, dil in DILATION_GROUPS)

F32 = jnp.float32
BF16 = jnp.bfloat16


def _dot(a, b):
    return jnp.dot(a, b, preferred_element_type=F32)


def _dot_nt(a, b):
    return lax.dot_general(a, b, (((1,), (1,)), ((), ())), preferred_element_type=F32)


def _params(*sem):
    return pltpu.CompilerParams(dimension_semantics=sem, vmem_limit_bytes=VMEM_LIMIT)


def _rmsnorm_kernel(*refs):
    x_refs, (g_ref, o_ref) = refs[:-2], refs[-2:]
    x = _read_tile(x_refs)
    ms = jnp.mean(x * x, axis=-1, keepdims=True)
    o_ref[...] = (x * lax.rsqrt(ms + EPS) * g_ref[...]).astype(o_ref.dtype)


def _rmsnorm(x, g):
    x_specs, x_args = _row_tiles(x, D_MODEL)
    return pl.pallas_call(
        _rmsnorm_kernel,
        grid=(N_TILES,),
        in_specs=x_specs + [pl.BlockSpec((1, D_MODEL), lambda i: (0, 0))],
        out_specs=pl.BlockSpec((TM, D_MODEL), lambda i: (i, 0)),
        out_shape=jax.ShapeDtypeStruct((ROWS, D_MODEL), BF16),
        compiler_params=_params("arbitrary"),
        name="rmsnorm",
    )(*x_args, g.reshape(1, D_MODEL))


def _gelu(x):
    return 0.5 * x * (1.0 + lax.erf(x * np.float32(np.sqrt(0.5))))


def _silu(x):
    return x * jax.nn.sigmoid(x)


TM_E = 256
N_TILES_E = ROWS // TM_E
N_TILES_E_P = ROWS_P // TM_E


TILES_PER_SEQ_E = SEQ // TM_E
KEPT_TILES_E = min(W_MAX, SEQ) // TM_E


def _rope(x, cos, sin, o_ref, kv_ref, kt_s, kind, scale):
    lane = lax.broadcasted_iota(jnp.int32, (TM_E, LANES), 1)
    first_half = (lane % HD) < (HD // 2)
    for p in range(N_PAIRS):
        xc = x[:, p * LANES:(p + 1) * LANES]
        partner = jnp.where(first_half, pltpu.roll(xc, LANES - HD // 2, 1), pltpu.roll(xc, HD // 2, 1))
        out = xc * cos + partner * sin
        if scale is not None:
            out = out * scale
        o_ref[p, kind] = out
        if kind == K:
            kv_ref[:, p * LANES:(p + 1) * LANES] = out
            kt_s[p * LANES:(p + 1) * LANES, :] = out.T


def _even_in_kernel(n_alias, x_ref, w_ref, cos_ref, sin_ref, wm_ref, bias_ref, *refs):
    a_ref, av_ref, o_ref, kv_ref, kt_ref, vt_ref, kt_s, vt_s = refs[n_alias:]
    x = x_ref[...]
    col = lambda j: w_ref[:, j * D_MODEL:(j + 1) * D_MODEL]
    v = _gelu(_dot(x, col(1)))
    av_ref[...] = v
    v_bf = v.astype(BF16)
    u = _gelu(_dot(x, col(0)))
    z = _silu(_dot(x, col(2)))
    for c in range(TM_E // CHUNK):
        rows = slice(c * CHUNK, (c + 1) * CHUNK)
        for h in range(H_A):
            sl = slice(h * CH_A, (h + 1) * CH_A)
            mix = _dot(wm_ref[h], v_bf[rows, sl]) + bias_ref[:, sl]
            a_ref[rows, sl] = (u[rows, sl] * mix * z[rows, sl]).astype(a_ref.dtype)
    cos = cos_ref[...]
    sin = sin_ref[...]
    _rope(_dot(x, col(3)), cos, sin, o_ref, kv_ref, kt_s, Q, np.float32(HD ** -0.5))
    _rope(_dot(x, col(4)), cos, sin, o_ref, kv_ref, kt_s, K, None)
    v_b = _dot(x, col(5))
    kv_ref[:, W_B:2 * W_B] = v_b
    z_b = _silu(_dot(x, col(6)))
    for p in range(N_PAIRS):
        v_p = v_b[:, p * LANES:(p + 1) * LANES]
        o_ref[p, V] = v_p
        vt_s[p * LANES:(p + 1) * LANES, :] = v_p.T
        o_ref[p, Z] = z_b[:, p * LANES:(p + 1) * LANES]

    i = pl.program_id(0)

    @pl.when((i < N_TILES_E_P) & (i % TILES_PER_SEQ_E >= TILES_PER_SEQ_E - KEPT_TILES_E))
    def _():
        kt_ref[...] = kt_s[...]
        vt_ref[...] = vt_s[...]


def _even_in(n, w, cos, sin, wm, bias, e, kt_all, vt_all):
    table = lambda i: (jnp.where(i < N_TILES_E_P, i % TILES_PER_SEQ_E, TILES_PER_SEQ_E), 0)
    variant = lambda i: jnp.where(i < N_TILES_E_P, 0, 1)
    kept_block = lambda i: (e, jnp.minimum(i // TILES_PER_SEQ_E, BATCH - 1), 0,
                            jnp.where(i < N_TILES_E_P,
                                      jnp.maximum(i % TILES_PER_SEQ_E - (TILES_PER_SEQ_E - KEPT_TILES_E), 0),
                                      KEPT_TILES_E - 1))
    t_spec = pl.BlockSpec((None, None, W_B, TM_E), kept_block)
    t_shape = jax.ShapeDtypeStruct((N_EVEN, BATCH, W_B, KEPT_TILES_E * TM_E), F32)
    carried = [] if kt_all is None else [kt_all, vt_all]
    n_in = 6
    return pl.pallas_call(
        functools.partial(_even_in_kernel, len(carried)),
        grid=(N_TILES_E,),
        in_specs=[pl.BlockSpec((TM_E, D_MODEL), lambda i: (i, 0)),
                  pl.BlockSpec((D_MODEL, EVEN_IN), lambda i: (0, 0), pipeline_mode=pl.Buffered(1)),
                  pl.BlockSpec((TM_E, LANES), table),
                  pl.BlockSpec((TM_E, LANES), table),
                  pl.BlockSpec((None, H_A, CHUNK, CHUNK), lambda i: (variant(i), 0, 0, 0)),
                  pl.BlockSpec((None, CHUNK, W_A), lambda i: (variant(i), 0, 0))]
                 + [pl.BlockSpec(memory_space=pl.ANY)] * len(carried),
        out_specs=[pl.BlockSpec((TM_E, W_A), lambda i: (i, 0)),
                   pl.BlockSpec((TM_E, W_A), lambda i: (jnp.maximum(i - N_TILES_E_P, 0), 0)),
                   pl.BlockSpec((N_PAIRS, 4, TM_E, LANES), lambda i: (0, 0, i, 0)),
                   pl.BlockSpec((TM_E, 2 * W_B), lambda i: (jnp.maximum(i - N_TILES_E_P, 0), 0)),
                   t_spec, t_spec],
        out_shape=[jax.ShapeDtypeStruct((ROWS, W_A), BF16),
                   jax.ShapeDtypeStruct((ROWS_S, W_A), F32),
                   jax.ShapeDtypeStruct((N_PAIRS, 4, ROWS, LANES), F32),
                   jax.ShapeDtypeStruct((ROWS_S, 2 * W_B), F32),
                   t_shape, t_shape],
        scratch_shapes=[pltpu.VMEM((W_B, TM_E), F32)] * 2,
        input_output_aliases={n_in + j: 4 + j for j in range(len(carried))},
        compiler_params=_params("arbitrary"),
        name="even_in",
    )(n, w, cos, sin, wm, bias, *carried)


def _band_bias():
    i = np.arange(QB)[:, None]
    j = np.arange(2 * QB)[None, :]
    band = (j - i >= 0) & (j - i <= QB)
    first = band & (j >= QB)
    return np.where(np.stack([band, first]), 0.0, NEG).astype(np.float32)


STEPS_PER_UNIT = 8
HALF_BLOCKS = SEQ // QB // 2


def _prompt_attention_step(phase, x_ref, bias_ref, o_ref,
                           q0_s, q1_s, k_s, v_s, o0_s, o1_s, o2_s, lse0_s, lse1_s, lse2_s):
    o_refs = (o0_s, o1_s, o2_s)
    lse_refs = (lse0_s, lse1_s, lse2_s)
    head0 = lax.broadcasted_iota(jnp.int32, (QB, LANES), 1) < HD
    ones = jnp.ones((2 * QB, LANES), BF16)

    @pl.when(phase == 0)
    def _():
        k_s[0:QB, :] = jnp.zeros((QB, LANES), BF16)
        v_s[0:QB, :] = jnp.zeros((QB, LANES), BF16)

    for g, (_, dil) in enumerate(DILATION_GROUPS):
        n = SEQ // dil
        ch = min(n, 4 * QB)
        n_ch = n // ch
        n_blk = n // QB

        def deinterleave(idx, carry, dil=dil, ch=ch, n_ch=n_ch):
            start = idx // n_ch + dil * ch * (idx % n_ch)
            src = pl.ds(pl.multiple_of(start, ch), ch) if dil == 1 else pl.ds(start, ch, stride=dil)
            dst = pl.ds(pl.multiple_of(idx * ch, QB), ch)
            dst_pad = pl.ds(pl.multiple_of(QB + idx * ch, QB), ch)
            first = lax.broadcasted_iota(jnp.int32, (ch, LANES), 1) < HD
            q = x_ref[Q, src, :]
            q0_s[dst, :] = jnp.where(first, q, 0.0).astype(BF16)
            q1_s[dst, :] = jnp.where(first, 0.0, q).astype(BF16)
            k_s[dst_pad, :] = x_ref[K, src, :].astype(BF16)
            v_s[dst_pad, :] = x_ref[V, src, :].astype(BF16)
            return carry

        @pl.when(phase == 2 * g)
        def _(ch=ch, deinterleave=deinterleave):
            lax.fori_loop(0, SEQ // ch, deinterleave, 0)

        def body(i, carry, g=g, dil=dil, n_blk=n_blk):
            blk = (phase % 2) * HALF_BLOCKS + i
            res = blk // n_blk
            pos = blk % n_blk
            rows = pl.ds(pl.multiple_of(blk * QB, QB), QB)
            rows2 = pl.ds(pl.multiple_of(blk * QB, QB), 2 * QB)
            k2 = k_s[rows2, :]
            v_aug = jnp.concatenate([v_s[rows2, :], ones], axis=1)
            bias = bias_ref[jnp.where(pos > 0, 0, 1)]
            q2 = jnp.concatenate([q0_s[rows, :], q1_s[rows, :]], axis=0)
            s = _dot_nt(q2, k2) + jnp.concatenate([bias, bias], axis=0)
            m = jnp.max(s, axis=1, keepdims=True)
            r = _dot(jnp.exp(s - m).astype(BF16), v_aug)
            r = jnp.where(jnp.concatenate([head0, head0], axis=1), r[:QB], r[QB:])
            l = r[:, LANES:]
            start = res + dil * QB * pos
            dst = pl.ds(pl.multiple_of(start, QB), QB) if dil == 1 else pl.ds(start, QB, stride=dil)
            o_refs[g][dst, :] = r[:, :LANES] / l
            lse_refs[g][dst, :] = jnp.where(head0, m[:QB], m[QB:]) + jnp.log(l)
            return carry

        @pl.when(phase // 2 == g)
        def _(body=body):
            lax.fori_loop(0, HALF_BLOCKS, body, 0, unroll=16)

    def finish(i, carry):
        blk = (phase % 2) * HALF_BLOCKS + i
        rows = pl.ds(pl.multiple_of(blk * QB, QB), QB)
        lses = [ref[rows, :] for ref in lse_refs]
        top = jnp.maximum(jnp.maximum(lses[0], lses[1]), lses[2])
        wts = [jnp.exp(lse - top) for lse in lses]
        num = wts[0] * o0_s[rows, :] + wts[1] * o1_s[rows, :] + wts[2] * o2_s[rows, :]
        den = wts[0] + wts[1] + wts[2]
        o_ref[rows, :] = (num / den * x_ref[Z, rows, :]).astype(o_ref.dtype)
        return carry

    @pl.when(phase // 2 == len(DILATION_GROUPS))
    def _():
        lax.fori_loop(0, HALF_BLOCKS, finish, 0)


def _sample_key_counts():
    i = (np.arange(HEADS_PER_STEP * DEC_SEQ) % DEC_SEQ)[:, None]

    def count(dist):
        total = np.zeros(dist.shape, np.float32)
        for window, dil in DILATION_GROUPS:
            total += (dist >= 0) & (dist <= window) & (dist % dil == 0)
        return total

    cache = count(CACHE_LEN + i - np.arange(CACHE_LEN)[None, :])
    new = count(i - np.arange(QB)[None, :]) * (np.arange(QB)[None, :] < DEC_SEQ)
    return cache, new.astype(np.float32)


def _sample_attention_step(x_ref, kt_ref, vt_ref, cc_ref, cn_ref, o_ref):
    width = HEADS_PER_STEP * HD
    n_q = HEADS_PER_STEP * DEC_SEQ
    wide = lambda kind: jnp.concatenate([x_ref[p, kind] for p in range(PAIRS_PER_STEP)], axis=1)
    lane = lax.broadcasted_iota(jnp.int32, (n_q, width), 1)
    row = lax.broadcasted_iota(jnp.int32, (n_q, width), 0)
    own_head = (lane // HD) == (row // DEC_SEQ)
    q_rep = jnp.concatenate([wide(Q)] * HEADS_PER_STEP, axis=0)
    q_big = jnp.where(own_head, q_rep, 0.0).astype(BF16)
    pad = jnp.zeros((QB - DEC_SEQ, width), F32)
    kn = jnp.concatenate([wide(K), pad], axis=0).astype(BF16)
    vn = jnp.concatenate([wide(V), pad], axis=0).astype(BF16)
    cc = cc_ref[...]
    cn = cn_ref[...]
    kt = kt_ref[...].reshape(width, CACHE_LEN).astype(BF16)
    s_c = jnp.where(cc > 0.0, _dot(q_big, kt), NEG)
    s_n = jnp.where(cn > 0.0, _dot_nt(q_big, kn), NEG)
    m = jnp.maximum(jnp.max(s_c, axis=1, keepdims=True), jnp.max(s_n, axis=1, keepdims=True))
    e_c = jnp.exp(s_c - m) * cc
    e_n = jnp.exp(s_n - m) * cn
    l = jnp.sum(e_c, axis=1, keepdims=True) + jnp.sum(e_n, axis=1, keepdims=True)
    vt = vt_ref[...].reshape(width, CACHE_LEN).astype(BF16)
    out_big = (_dot_nt(e_c.astype(BF16), vt) + _dot(e_n.astype(BF16), vn)) / l
    out_big = jnp.where(own_head, out_big, 0.0)
    out = jnp.sum(out_big.reshape(HEADS_PER_STEP, DEC_SEQ, width), axis=0)
    o_ref[...] = (out * wide(Z)).astype(o_ref.dtype)


def _attn_kernel(xs_ref, kt_ref, vt_ref, cc_ref, cn_ref, xp_ref, bias_ref, os_ref, op_ref, *scratch):
    _sample_attention_step(xs_ref, kt_ref, vt_ref, cc_ref, cn_ref, os_ref)
    _prompt_attention_step(pl.program_id(0) % STEPS_PER_UNIT, xp_ref, bias_ref, op_ref, *scratch)


def _attn(x, kt_all, vt_all, e, cc, cn, bias):
    first = ROWS_P // DEC_SEQ
    width = HEADS_PER_STEP * HD
    per_row = H_B // HEADS_PER_STEP
    n_steps = DEC_BATCH * per_row
    assert n_steps == BATCH * N_PAIRS * STEPS_PER_UNIT
    n_q = HEADS_PER_STEP * DEC_SEQ
    once = pl.Buffered(1)
    unit = lambda s: s // STEPS_PER_UNIT
    cache_spec = pl.BlockSpec((None, None, HEADS_PER_STEP, HD, CACHE_LEN),
                              lambda s: (e, s // per_row, s % per_row, 0, 0))
    return pl.pallas_call(
        _attn_kernel,
        grid=(n_steps,),
        in_specs=[pl.BlockSpec((PAIRS_PER_STEP, 4, DEC_SEQ, LANES), lambda s: (s % per_row, 0, first + s // per_row, 0)),
                  cache_spec, cache_spec,
                  pl.BlockSpec((n_q, CACHE_LEN), lambda s: (0, 0), pipeline_mode=once),
                  pl.BlockSpec((n_q, QB), lambda s: (0, 0), pipeline_mode=once),
                  pl.BlockSpec((None, 4, SEQ, LANES), lambda s: (unit(s) % N_PAIRS, 0, unit(s) // N_PAIRS, 0)),
                  pl.BlockSpec((2, QB, 2 * QB), lambda s: (0, 0, 0), pipeline_mode=once)],
        out_specs=[pl.BlockSpec((DEC_SEQ, width), lambda s: (s // per_row, s % per_row)),
                   pl.BlockSpec((SEQ, LANES), lambda s: (unit(s) // N_PAIRS, unit(s) % N_PAIRS))],
        out_shape=[jax.ShapeDtypeStruct((ROWS_S, W_B), BF16),
                   jax.ShapeDtypeStruct((ROWS_P, W_B), BF16)],
        scratch_shapes=[pltpu.VMEM((SEQ, LANES), BF16)] * 2
                       + [pltpu.VMEM((QB + SEQ, LANES), BF16)] * 2
                       + [pltpu.VMEM((SEQ, LANES), F32)] * 6,
        compiler_params=pltpu.CompilerParams(dimension_semantics=("arbitrary",),
                                             vmem_limit_bytes=ATTN_VMEM_LIMIT),
        name="attn",
    )(x, kt_all, vt_all, cc, cn, x, bias)


def _row_tiles(x, width, col=0):
    if not isinstance(x, tuple):
        return [pl.BlockSpec((TM, width), lambda i: (i, col))], [x]
    x_p, x_s = x
    return ([pl.BlockSpec((TM, width), lambda i: (jnp.minimum(i, N_TILES_P - 1), col)),
             pl.BlockSpec((TM, width), lambda i: (jnp.maximum(i - N_TILES_P, 0), col))],
            [x_p, x_s])


def _read_tile(refs):
    if len(refs) == 1:
        return refs[0][...]
    return jnp.where(pl.program_id(0) < N_TILES_P, refs[0][...], refs[1][...])


def _out_proj_kernel(n_b, n_h, a_ref, *refs):
    b_refs, refs = refs[:n_b], refs[n_b:]
    w_ref, refs = refs[0], refs[1:]
    h_refs, (g_ref, h_out_ref, n_out_ref) = refs[:n_h], refs[n_h:]
    half = w_ref.shape[0] // 2
    h_new = (_read_tile(h_refs) + _dot(a_ref[...], w_ref[0:half, :])
             + _dot(_read_tile(b_refs), w_ref[half:, :]))
    h_out_ref[...] = h_new
    ms = jnp.mean(h_new * h_new, axis=-1, keepdims=True)
    n_out_ref[...] = (h_new * lax.rsqrt(ms + EPS) * g_ref[...]).astype(n_out_ref.dtype)


def _out_proj(a, a_col, b, b_col, w, h, g, norm_dtype):
    half = w.shape[0] // 2
    b_specs, b_args = _row_tiles(b, half, b_col)
    h_specs, h_args = _row_tiles(h, D_MODEL)
    return pl.pallas_call(
        functools.partial(_out_proj_kernel, len(b_args), len(h_args)),
        grid=(N_TILES,),
        in_specs=[pl.BlockSpec((TM, half), lambda i: (i, a_col))] + b_specs
                 + [pl.BlockSpec((2 * half, D_MODEL), lambda i: (0, 0))] + h_specs
                 + [pl.BlockSpec((1, D_MODEL), lambda i: (0, 0))],
        out_specs=[pl.BlockSpec((TM, D_MODEL), lambda i: (i, 0)),
                   pl.BlockSpec((TM, D_MODEL), lambda i: (i, 0))],
        out_shape=[jax.ShapeDtypeStruct((ROWS, D_MODEL), F32),
                   jax.ShapeDtypeStruct((ROWS, D_MODEL), norm_dtype)],
        compiler_params=_params("arbitrary"),
        name="out_proj",
    )(a, *b_args, w, *h_args, g.reshape(1, D_MODEL))


TN_ODD = 512
N_COL_ODD = W_C // TN_ODD
PRE_SLOTS = BATCH + ROWS_S // TM


def _odd_in_kernel(x_ref, wb_ref, wc_ref, wx_ref, wz_ref, cw_ref, e1_ref, e2_ref, y_ref, pre_ref, buf):
    i = pl.program_id(1)

    @pl.when(i == 0)
    def _():
        buf[0:SUBLANES, :] = jnp.zeros((SUBLANES, TN_ODD), F32)

    x = x_ref[...]
    pre = _dot(x, wc_ref[...]) * _dot(x, wx_ref[...])
    pre_ref[...] = pre
    buf[SUBLANES:SUBLANES + TM, :] = pre
    row = lax.broadcasted_iota(jnp.int32, (TM, TN_ODD), 0)
    t = jnp.where(i < N_TILES_P, (i % TILES_PER_SEQ) * TM + row, row % DEC_SEQ)
    is_sample = i >= N_TILES_P
    pre_m1 = jnp.where(t >= 1, buf[SUBLANES - 1:SUBLANES - 1 + TM, :], jnp.where(is_sample, e1_ref[...], 0.0))
    pre_m2 = jnp.where(t >= 2, buf[SUBLANES - 2:SUBLANES - 2 + TM, :], jnp.where(is_sample, e2_ref[...], 0.0))
    conv = cw_ref[0:1, :] * pre_m2 + cw_ref[1:2, :] * pre_m1 + cw_ref[2:3, :] * pre
    y = _dot(x, wb_ref[...]) * conv * _silu(_dot(x, wz_ref[...]))
    y_ref[...] = y.astype(y_ref.dtype)
    buf[0:SUBLANES, :] = pre[TM - SUBLANES:, :]


def _odd_in(n, w, conv_w, e1, e2):
    wcol = lambda part: (lambda c, i: (0, part * N_COL_ODD + c))
    e_block = lambda c, i: (jnp.maximum(i - N_TILES_P, 0), c)
    pre_block = lambda c, i: (jnp.where(i < N_TILES_P, i // TILES_PER_SEQ, i - N_TILES_P + BATCH), c)
    return pl.pallas_call(
        _odd_in_kernel,
        grid=(N_COL_ODD, N_TILES),
        in_specs=[pl.BlockSpec((TM, D_MODEL), lambda c, i: (i, 0)),
                  pl.BlockSpec((D_MODEL, TN_ODD), wcol(0)),
                  pl.BlockSpec((D_MODEL, TN_ODD), wcol(1)),
                  pl.BlockSpec((D_MODEL, TN_ODD), wcol(2)),
                  pl.BlockSpec((D_MODEL, TN_ODD), wcol(3)),
                  pl.BlockSpec((CONV_W, TN_ODD), lambda c, i: (0, c)),
                  pl.BlockSpec((TM, TN_ODD), e_block),
                  pl.BlockSpec((TM, TN_ODD), e_block)],
        out_specs=[pl.BlockSpec((TM, TN_ODD), lambda c, i: (i, c)),
                   pl.BlockSpec((TM, TN_ODD), pre_block)],
        out_shape=[jax.ShapeDtypeStruct((ROWS, W_C), BF16),
                   jax.ShapeDtypeStruct((PRE_SLOTS * TM, W_C), F32)],
        scratch_shapes=[pltpu.VMEM((SUBLANES + TM, TN_ODD), F32)],
        compiler_params=_params("arbitrary", "arbitrary"),
        name="odd_in",
    )(n, w, w, w, w, conv_w, e1, e2)


def _rope_tables():
    half = HD // 2
    inv = 1.0 / (ROPE_THETA ** (jnp.arange(half, dtype=F32) / half))
    pos_s = PAST_LEN + (jnp.arange(TM_E) % DEC_SEQ)
    pos = jnp.concatenate([jnp.arange(SEQ), pos_s]).astype(F32)
    ang = pos[:, None] * inv[None, :]
    cos = jnp.cos(ang)
    sin = jnp.sin(ang)
    reps = LANES // HD
    return (jnp.tile(jnp.concatenate([cos, cos], axis=1), (1, reps)),
            jnp.tile(jnp.concatenate([-sin, sin], axis=1), (1, reps)))


def _gmlp_weights(w_s, b_s):
    tril = jnp.tril(jnp.ones((CHUNK, CHUNK), bool))
    w_p = jnp.where(tril[None], w_s, 0.0)
    small = w_p[:, :DEC_SEQ, :DEC_SEQ]
    eye = jnp.eye(CHUNK // DEC_SEQ, dtype=F32)
    w_smp = jnp.einsum('ab,hts->hatbs', eye, small).reshape(H_A, CHUNK, CHUNK)
    wm = jnp.stack([w_p, w_smp]).astype(BF16)
    bias_p = jnp.repeat(b_s.T, CH_A, axis=1)
    bias_s = jnp.tile(bias_p[:DEC_SEQ], (CHUNK // DEC_SEQ, 1))
    return wm, jnp.stack([bias_p, bias_s])


def _conv_edges(state):
    e1 = jnp.pad(state[:, CONV_W - 2:], ((0, 0), (0, DEC_SEQ - 1), (0, 0)))
    e2 = jnp.pad(state, ((0, 0), (0, DEC_SEQ - (CONV_W - 1)), (0, 0)))
    return e1.reshape(ROWS_S, W_C), e2.reshape(ROWS_S, W_C)


def _prompt_cache_rows(t_all):
    return jnp.transpose(t_all.reshape(N_EVEN, BATCH, H_B, HD, t_all.shape[-1]), (0, 1, 4, 2, 3))


def kernel(x_prompt, x_sample, cache_b_k, cache_b_v, state_c_conv, norm_w, final_norm_w,
           w_in_even, w_s, b_s, w_out_even, w_in_odd, conv_w, w_out_odd):
    h = (x_prompt.reshape(ROWS_P, D_MODEL), x_sample.reshape(ROWS_S, D_MODEL))
    cos, sin = _rope_tables()
    band_bias = jnp.asarray(_band_bias())
    cc, cn = (jnp.asarray(c) for c in _sample_key_counts())
    kt_all = jnp.transpose(cache_b_k, (0, 1, 3, 4, 2))
    vt_all = jnp.transpose(cache_b_v, (0, 1, 3, 4, 2))
    n = _rmsnorm(h, norm_w[0])
    ks_l, vs_l, av_l, cp_l, cs_l = [], [], [], [], []
    new_kt = new_vt = None
    for layer in range(DEPTH):
        last = layer == DEPTH - 1
        g_next = final_norm_w if last else norm_w[layer + 1]
        norm_dtype = F32 if last else BF16
        if layer % 2 == 0:
            e = layer // 2
            wm, bias = _gmlp_weights(w_s[e], b_s[e])
            a_out, av, att_in, kv, new_kt, new_vt = _even_in(n, w_in_even[e].astype(BF16), cos, sin, wm, bias,
                                                             e, new_kt, new_vt)
            att_s, att_p = _attn(att_in, kt_all, vt_all, e, cc, cn, band_bias)
            h, n = _out_proj(a_out, 0, (att_p, att_s), 0, w_out_even[e].astype(BF16), h, g_next, norm_dtype)
            ks_l.append(kv[:, :W_B].reshape(DEC_BATCH, DEC_SEQ, H_B, HD))
            vs_l.append(kv[:, W_B:].reshape(DEC_BATCH, DEC_SEQ, H_B, HD))
            av_l.append(av)
        else:
            c = layer // 2
            e1, e2 = _conv_edges(state_c_conv[c])
            y, pre = _odd_in(n, w_in_odd[c].astype(BF16), conv_w[c], e1, e2)
            h, n = _out_proj(y, 0, y, 1, w_out_odd[c].astype(BF16), h, g_next, norm_dtype)
            pre_p = pre[:BATCH * TM].reshape(BATCH, TM, W_C)
            cp_l.append(pre_p[:, TM - (CONV_W - 1):])
            cs_l.append(pre[BATCH * TM:].reshape(DEC_BATCH, DEC_SEQ, W_C)[:, DEC_SEQ - (CONV_W - 1):])
    y_prompt = n[:ROWS_P].reshape(BATCH, SEQ, D_MODEL)
    y_sample = n[ROWS_P:].reshape(DEC_BATCH, DEC_SEQ, D_MODEL)
    new_a_v_sample = jnp.stack(av_l).reshape(N_EVEN, DEC_BATCH, DEC_SEQ, W_A)
    return (y_prompt, y_sample, _prompt_cache_rows(new_kt), _prompt_cache_rows(new_vt), jnp.stack(ks_l), jnp.stack(vs_l),
            new_a_v_sample, jnp.stack(cp_l), jnp.stack(cs_l))
```

```python
import functools

import numpy as np
import jax
import jax.numpy as jnp
from jax import lax
from jax.experimental import pallas as pl
from jax.experimental.pallas import tpu as pltpu

D_MODEL = 1024
BATCH = 4
SEQ = 4096
DEPTH = 4
DEC_BATCH = 128
DEC_SEQ = 8
PAST_LEN = 2048
N_EVEN = (DEPTH + 1) // 2
N_ODD = DEPTH // 2
W_A = D_MODEL
CHUNK = 128
H_A = 8
CH_A = W_A // H_A
W_B = D_MODEL
H_B = 16
HD = W_B // H_B
DILATION_GROUPS = ((128, 1), (512, 4), (2048, 16))
W_MAX = 2048
ROPE_THETA = 10000.0
W_C = 2 * D_MODEL
CONV_W = 3
EPS = 1e-6
NEG = -1e30
EVEN_IN = 3 * W_A + 4 * W_B
ODD_IN = 4 * W_C

LANES = 128
SUBLANES = 8
ROWS_P = BATCH * SEQ
ROWS_S = DEC_BATCH * DEC_SEQ
ROWS = ROWS_P + ROWS_S
TM = 512
N_TILES = ROWS // TM
N_TILES_P = ROWS_P // TM
TILES_PER_SEQ = SEQ // TM
QB = 128
CACHE_LEN = min(W_MAX, PAST_LEN)
HEADS_PER_STEP = 8
N_PAIRS = W_B // LANES
PAIRS_PER_STEP = HEADS_PER_STEP * HD // LANES
Q, K, V, Z = range(4)
VMEM_LIMIT = 48 * 1024 * 1024
ATTN_VMEM_LIMIT = 60 * 1024 * 1024

assert all(window // dil == QB for window, dil in DILATION_GROUPS)

F32 = jnp.float32
BF16 = jnp.bfloat16


def _dot(a, b):
    return jnp.dot(a, b, preferred_element_type=F32)


def _dot_nt(a, b):
    return lax.dot_general(a, b, (((1,), (1,)), ((), ())), preferred_element_type=F32)


def _params(*sem):
    return pltpu.CompilerParams(dimension_semantics=sem, vmem_limit_bytes=VMEM_LIMIT)


def _rmsnorm_kernel(*refs):
    x_refs, (g_ref, o_ref) = refs[:-2], refs[-2:]
    x = _read_tile(x_refs)
    ms = jnp.mean(x * x, axis=-1, keepdims=True)
    o_ref[...] = (x * lax.rsqrt(ms + EPS) * g_ref[...]).astype(o_ref.dtype)


def _rmsnorm(x, g):
    x_specs, x_args = _row_tiles(x, D_MODEL)
    return pl.pallas_call(
        _rmsnorm_kernel,
        grid=(N_TILES,),
        in_specs=x_specs + [pl.BlockSpec((1, D_MODEL), lambda i: (0, 0))],
        out_specs=pl.BlockSpec((TM, D_MODEL), lambda i: (i, 0)),
        out_shape=jax.ShapeDtypeStruct((ROWS, D_MODEL), BF16),
        compiler_params=_params("arbitrary"),
        name="rmsnorm",
    )(*x_args, g.reshape(1, D_MODEL))


def _gelu(x):
    return 0.5 * x * (1.0 + lax.erf(x * np.float32(np.sqrt(0.5))))


def _silu(x):
    return x * jax.nn.sigmoid(x)


TM_E = 256
N_TILES_E = ROWS // TM_E
N_TILES_E_P = ROWS_P // TM_E


TILES_PER_SEQ_E = SEQ // TM_E
KEPT_TILES_E = min(W_MAX, SEQ) // TM_E


def _rope(x, cos, sin, o_ref, kv_ref, kt_s, kind, scale):
    lane = lax.broadcasted_iota(jnp.int32, (TM_E, LANES), 1)
    first_half = (lane % HD) < (HD // 2)
    for p in range(N_PAIRS):
        xc = x[:, p * LANES:(p + 1) * LANES]
        partner = jnp.where(first_half, pltpu.roll(xc, LANES - HD // 2, 1), pltpu.roll(xc, HD // 2, 1))
        out = xc * cos + partner * sin
        if scale is not None:
            out = out * scale
        o_ref[p, kind] = out
        if kind == K:
            kv_ref[:, p * LANES:(p + 1) * LANES] = out
            kt_s[p * LANES:(p + 1) * LANES, :] = out.T


def _even_in_kernel(n_alias, x_ref, w_ref, cos_ref, sin_ref, wm_ref, bias_ref, *refs):
    a_ref, av_ref, o_ref, kv_ref, kt_ref, vt_ref, kt_s, vt_s = refs[n_alias:]
    x = x_ref[...]
    col = lambda j: w_ref[:, j * D_MODEL:(j + 1) * D_MODEL]
    v = _gelu(_dot(x, col(1)))
    av_ref[...] = v
    v_bf = v.astype(BF16)
    u = _gelu(_dot(x, col(0)))
    z = _silu(_dot(x, col(2)))
    for c in range(TM_E // CHUNK):
        rows = slice(c * CHUNK, (c + 1) * CHUNK)
        for h in range(H_A):
            sl = slice(h * CH_A, (h + 1) * CH_A)
            mix = _dot(wm_ref[h], v_bf[rows, sl]) + bias_ref[:, sl]
            a_ref[rows, sl] = (u[rows, sl] * mix * z[rows, sl]).astype(a_ref.dtype)
    cos = cos_ref[...]
    sin = sin_ref[...]
    _rope(_dot(x, col(3)), cos, sin, o_ref, kv_ref, kt_s, Q, np.float32(HD ** -0.5))
    _rope(_dot(x, col(4)), cos, sin, o_ref, kv_ref, kt_s, K, None)
    v_b = _dot(x, col(5))
    kv_ref[:, W_B:2 * W_B] = v_b
    z_b = _silu(_dot(x, col(6)))
    for p in range(N_PAIRS):
        v_p = v_b[:, p * LANES:(p + 1) * LANES]
        o_ref[p, V] = v_p
        vt_s[p * LANES:(p + 1) * LANES, :] = v_p.T
        o_ref[p, Z] = z_b[:, p * LANES:(p + 1) * LANES]

    i = pl.program_id(0)

    @pl.when((i < N_TILES_E_P) & (i % TILES_PER_SEQ_E >= TILES_PER_SEQ_E - KEPT_TILES_E))
    def _():
        kt_ref[...] = kt_s[...]
        vt_ref[...] = vt_s[...]


def _even_in(n, w, cos, sin, wm, bias, e, kt_all, vt_all):
    table = lambda i: (jnp.where(i < N_TILES_E_P, i % TILES_PER_SEQ_E, TILES_PER_SEQ_E), 0)
    variant = lambda i: jnp.where(i < N_TILES_E_P, 0, 1)
    kept_block = lambda i: (e, jnp.minimum(i // TILES_PER_SEQ_E, BATCH - 1), 0,
                            jnp.where(i < N_TILES_E_P,
                                      jnp.maximum(i % TILES_PER_SEQ_E - (TILES_PER_SEQ_E - KEPT_TILES_E), 0),
                                      KEPT_TILES_E - 1))
    t_spec = pl.BlockSpec((None, None, W_B, TM_E), kept_block)
    t_shape = jax.ShapeDtypeStruct((N_EVEN, BATCH, W_B, KEPT_TILES_E * TM_E), F32)
    carried = [] if kt_all is None else [kt_all, vt_all]
    n_in = 6
    return pl.pallas_call(
        functools.partial(_even_in_kernel, len(carried)),
        grid=(N_TILES_E,),
        in_specs=[pl.BlockSpec((TM_E, D_MODEL), lambda i: (i, 0)),
                  pl.BlockSpec((None, D_MODEL, EVEN_IN), lambda i: (e, 0, 0), pipeline_mode=pl.Buffered(1)),
                  pl.BlockSpec((TM_E, LANES), table),
                  pl.BlockSpec((TM_E, LANES), table),
                  pl.BlockSpec((None, H_A, CHUNK, CHUNK), lambda i: (variant(i), 0, 0, 0)),
                  pl.BlockSpec((None, CHUNK, W_A), lambda i: (variant(i), 0, 0))]
                 + [pl.BlockSpec(memory_space=pl.ANY)] * len(carried),
        out_specs=[pl.BlockSpec((TM_E, W_A), lambda i: (i, 0)),
                   pl.BlockSpec((TM_E, W_A), lambda i: (jnp.maximum(i - N_TILES_E_P, 0), 0)),
                   pl.BlockSpec((N_PAIRS, 4, TM_E, LANES), lambda i: (0, 0, i, 0)),
                   pl.BlockSpec((TM_E, 2 * W_B), lambda i: (jnp.maximum(i - N_TILES_E_P, 0), 0)),
                   t_spec, t_spec],
        out_shape=[jax.ShapeDtypeStruct((ROWS, W_A), BF16),
                   jax.ShapeDtypeStruct((ROWS_S, W_A), F32),
                   jax.ShapeDtypeStruct((N_PAIRS, 4, ROWS, LANES), F32),
                   jax.ShapeDtypeStruct((ROWS_S, 2 * W_B), F32),
                   t_shape, t_shape],
        scratch_shapes=[pltpu.VMEM((W_B, TM_E), F32)] * 2,
        input_output_aliases={n_in + j: 4 + j for j in range(len(carried))},
        compiler_params=_params("arbitrary"),
        name="even_in",
    )(n, w, cos, sin, wm, bias, *carried)


def _band_bias():
    i = np.arange(QB)[:, None]
    j = np.arange(2 * QB)[None, :]
    band = (j - i >= 0) & (j - i <= QB)
    first = band & (j >= QB)
    return np.where(np.stack([band, first]), 0.0, NEG).astype(np.float32)


STEPS_PER_UNIT = 8
HALF_BLOCKS = SEQ // QB // 2


def _prompt_attention_step(phase, x_ref, bias_ref, o_ref,
                           q0_s, q1_s, k_s, v_s, o0_s, o1_s, o2_s, lse0_s, lse1_s, lse2_s):
    o_refs = (o0_s, o1_s, o2_s)
    lse_refs = (lse0_s, lse1_s, lse2_s)
    head0 = lax.broadcasted_iota(jnp.int32, (QB, LANES), 1) < HD
    ones = jnp.ones((2 * QB, LANES), BF16)

    @pl.when(phase == 0)
    def _():
        k_s[0:QB, :] = jnp.zeros((QB, LANES), BF16)
        v_s[0:QB, :] = jnp.zeros((QB, LANES), BF16)

    for g, (_, dil) in enumerate(DILATION_GROUPS):
        n = SEQ // dil
        ch = min(n, 4 * QB)
        n_ch = n // ch
        n_blk = n // QB

        def deinterleave(idx, carry, dil=dil, ch=ch, n_ch=n_ch):
            start = idx // n_ch + dil * ch * (idx % n_ch)
            src = pl.ds(pl.multiple_of(start, ch), ch) if dil == 1 else pl.ds(start, ch, stride=dil)
            dst = pl.ds(pl.multiple_of(idx * ch, QB), ch)
            dst_pad = pl.ds(pl.multiple_of(QB + idx * ch, QB), ch)
            first = lax.broadcasted_iota(jnp.int32, (ch, LANES), 1) < HD
            q = x_ref[Q, src, :]
            q0_s[dst, :] = jnp.where(first, q, 0.0).astype(BF16)
            q1_s[dst, :] = jnp.where(first, 0.0, q).astype(BF16)
            k_s[dst_pad, :] = x_ref[K, src, :].astype(BF16)
            v_s[dst_pad, :] = x_ref[V, src, :].astype(BF16)
            return carry

        @pl.when(phase == 2 * g)
        def _(ch=ch, deinterleave=deinterleave):
            lax.fori_loop(0, SEQ // ch, deinterleave, 0)

        def body(i, carry, g=g, dil=dil, n_blk=n_blk):
            blk = (phase % 2) * HALF_BLOCKS + i
            res = blk // n_blk
            pos = blk % n_blk
            rows = pl.ds(pl.multiple_of(blk * QB, QB), QB)
            rows2 = pl.ds(pl.multiple_of(blk * QB, QB), 2 * QB)
            k2 = k_s[rows2, :]
            v_aug = jnp.concatenate([v_s[rows2, :], ones], axis=1)
            bias = bias_ref[jnp.where(pos > 0, 0, 1)]
            q2 = jnp.concatenate([q0_s[rows, :], q1_s[rows, :]], axis=0)
            s = _dot_nt(q2, k2) + jnp.concatenate([bias, bias], axis=0)
            m = jnp.max(s, axis=1, keepdims=True)
            r = _dot(jnp.exp(s - m).astype(BF16), v_aug)
            r = jnp.where(jnp.concatenate([head0, head0], axis=1), r[:QB], r[QB:])
            l = r[:, LANES:]
            start = res + dil * QB * pos
            dst = pl.ds(pl.multiple_of(start, QB), QB) if dil == 1 else pl.ds(start, QB, stride=dil)
            o_refs[g][dst, :] = r[:, :LANES] / l
            lse_refs[g][dst, :] = jnp.where(head0, m[:QB], m[QB:]) + jnp.log(l)
            return carry

        @pl.when(phase // 2 == g)
        def _(body=body):
            lax.fori_loop(0, HALF_BLOCKS, body, 0, unroll=16)

    def finish(i, carry):
        blk = (phase % 2) * HALF_BLOCKS + i
        rows = pl.ds(pl.multiple_of(blk * QB, QB), QB)
        lses = [ref[rows, :] for ref in lse_refs]
        top = jnp.maximum(jnp.maximum(lses[0], lses[1]), lses[2])
        wts = [jnp.exp(lse - top) for lse in lses]
        num = wts[0] * o0_s[rows, :] + wts[1] * o1_s[rows, :] + wts[2] * o2_s[rows, :]
        den = wts[0] + wts[1] + wts[2]
        o_ref[rows, :] = (num / den * x_ref[Z, rows, :]).astype(o_ref.dtype)
        return carry

    @pl.when(phase // 2 == len(DILATION_GROUPS))
    def _():
        lax.fori_loop(0, HALF_BLOCKS, finish, 0)


def _sample_key_counts():
    i = (np.arange(HEADS_PER_STEP * DEC_SEQ) % DEC_SEQ)[:, None]

    def count(dist):
        total = np.zeros(dist.shape, np.float32)
        for window, dil in DILATION_GROUPS:
            total += (dist >= 0) & (dist <= window) & (dist % dil == 0)
        return total

    cache = count(CACHE_LEN + i - np.arange(CACHE_LEN)[None, :])
    new = count(i - np.arange(QB)[None, :]) * (np.arange(QB)[None, :] < DEC_SEQ)
    return cache, new.astype(np.float32)


def _sample_attention_step(x_ref, kt_ref, vt_ref, cc_ref, cn_ref, o_ref):
    width = HEADS_PER_STEP * HD
    n_q = HEADS_PER_STEP * DEC_SEQ
    wide = lambda kind: jnp.concatenate([x_ref[p, kind] for p in range(PAIRS_PER_STEP)], axis=1)
    lane = lax.broadcasted_iota(jnp.int32, (n_q, width), 1)
    row = lax.broadcasted_iota(jnp.int32, (n_q, width), 0)
    own_head = (lane // HD) == (row // DEC_SEQ)
    q_rep = jnp.concatenate([wide(Q)] * HEADS_PER_STEP, axis=0)
    q_big = jnp.where(own_head, q_rep, 0.0).astype(BF16)
    pad = jnp.zeros((QB - DEC_SEQ, width), F32)
    kn = jnp.concatenate([wide(K), pad], axis=0).astype(BF16)
    vn = jnp.concatenate([wide(V), pad], axis=0).astype(BF16)
    cc = cc_ref[...]
    cn = cn_ref[...]
    kt = kt_ref[...].reshape(width, CACHE_LEN).astype(BF16)
    s_c = jnp.where(cc > 0.0, _dot(q_big, kt), NEG)
    s_n = jnp.where(cn > 0.0, _dot_nt(q_big, kn), NEG)
    m = jnp.maximum(jnp.max(s_c, axis=1, keepdims=True), jnp.max(s_n, axis=1, keepdims=True))
    e_c = jnp.exp(s_c - m) * cc
    e_n = jnp.exp(s_n - m) * cn
    l = jnp.sum(e_c, axis=1, keepdims=True) + jnp.sum(e_n, axis=1, keepdims=True)
    vt = vt_ref[...].reshape(width, CACHE_LEN).astype(BF16)
    out_big = (_dot_nt(e_c.astype(BF16), vt) + _dot(e_n.astype(BF16), vn)) / l
    out_big = jnp.where(own_head, out_big, 0.0)
    out = jnp.sum(out_big.reshape(HEADS_PER_STEP, DEC_SEQ, width), axis=0)
    o_ref[...] = (out * wide(Z)).astype(o_ref.dtype)


def _attn_kernel(xs_ref, kt_ref, vt_ref, cc_ref, cn_ref, xp_ref, bias_ref, os_ref, op_ref, *scratch):
    _sample_attention_step(xs_ref, kt_ref, vt_ref, cc_ref, cn_ref, os_ref)
    _prompt_attention_step(pl.program_id(0) % STEPS_PER_UNIT, xp_ref, bias_ref, op_ref, *scratch)


def _attn(x, kt_all, vt_all, e, cc, cn, bias):
    first = ROWS_P // DEC_SEQ
    width = HEADS_PER_STEP * HD
    per_row = H_B // HEADS_PER_STEP
    n_steps = DEC_BATCH * per_row
    assert n_steps == BATCH * N_PAIRS * STEPS_PER_UNIT
    n_q = HEADS_PER_STEP * DEC_SEQ
    once = pl.Buffered(1)
    unit = lambda s: s // STEPS_PER_UNIT
    cache_spec = pl.BlockSpec((None, None, HEADS_PER_STEP, HD, CACHE_LEN),
                              lambda s: (e, s // per_row, s % per_row, 0, 0))
    return pl.pallas_call(
        _attn_kernel,
        grid=(n_steps,),
        in_specs=[pl.BlockSpec((PAIRS_PER_STEP, 4, DEC_SEQ, LANES), lambda s: (s % per_row, 0, first + s // per_row, 0)),
                  cache_spec, cache_spec,
                  pl.BlockSpec((n_q, CACHE_LEN), lambda s: (0, 0), pipeline_mode=once),
                  pl.BlockSpec((n_q, QB), lambda s: (0, 0), pipeline_mode=once),
                  pl.BlockSpec((None, 4, SEQ, LANES), lambda s: (unit(s) % N_PAIRS, 0, unit(s) // N_PAIRS, 0)),
                  pl.BlockSpec((2, QB, 2 * QB), lambda s: (0, 0, 0), pipeline_mode=once)],
        out_specs=[pl.BlockSpec((DEC_SEQ, width), lambda s: (s // per_row, s % per_row)),
                   pl.BlockSpec((SEQ, LANES), lambda s: (unit(s) // N_PAIRS, unit(s) % N_PAIRS))],
        out_shape=[jax.ShapeDtypeStruct((ROWS_S, W_B), BF16),
                   jax.ShapeDtypeStruct((ROWS_P, W_B), BF16)],
        scratch_shapes=[pltpu.VMEM((SEQ, LANES), BF16)] * 2
                       + [pltpu.VMEM((QB + SEQ, LANES), BF16)] * 2
                       + [pltpu.VMEM((SEQ, LANES), F32)] * 6,
        compiler_params=pltpu.CompilerParams(dimension_semantics=("arbitrary",),
                                             vmem_limit_bytes=ATTN_VMEM_LIMIT),
        name="attn",
    )(x, kt_all, vt_all, cc, cn, x, bias)


def _row_tiles(x, width, col=0):
    if not isinstance(x, tuple):
        return [pl.BlockSpec((TM, width), lambda i: (i, col))], [x]
    x_p, x_s = x
    return ([pl.BlockSpec((TM, width), lambda i: (jnp.minimum(i, N_TILES_P - 1), col)),
             pl.BlockSpec((TM, width), lambda i: (jnp.maximum(i - N_TILES_P, 0), col))],
            [x_p, x_s])


def _read_tile(refs):
    if len(refs) == 1:
        return refs[0][...]
    return jnp.where(pl.program_id(0) < N_TILES_P, refs[0][...], refs[1][...])


def _out_proj_kernel(n_b, n_h, final, a_ref, *refs):
    b_refs, refs = refs[:n_b], refs[n_b:]
    w_ref, refs = refs[0], refs[1:]
    h_refs, (g_ref, *out_refs) = refs[:n_h], refs[n_h:]
    half = w_ref.shape[0] // 2
    h_new = (_read_tile(h_refs) + _dot(a_ref[...], w_ref[0:half, :])
             + _dot(_read_tile(b_refs), w_ref[half:, :]))
    ms = jnp.mean(h_new * h_new, axis=-1, keepdims=True)
    normed = h_new * lax.rsqrt(ms + EPS) * g_ref[...]
    if not final:
        h_out_ref, n_out_ref = out_refs
        h_out_ref[...] = h_new
        n_out_ref[...] = normed.astype(n_out_ref.dtype)
        return
    y_p_ref, y_s_ref = out_refs
    y_s_ref[...] = normed

    @pl.when(pl.program_id(0) < N_TILES_P)
    def _():
        y_p_ref[...] = normed


def _out_proj(a, a_col, b, b_col, w, e, h, g, final):
    half = w.shape[1] // 2
    b_specs, b_args = _row_tiles(b, half, b_col)
    h_specs, h_args = _row_tiles(h, D_MODEL)
    if final:
        out_specs = [pl.BlockSpec((TM, D_MODEL), lambda i: (jnp.minimum(i, N_TILES_P - 1), 0)),
                     pl.BlockSpec((TM, D_MODEL), lambda i: (jnp.maximum(i - N_TILES_P, 0), 0))]
        out_shape = [jax.ShapeDtypeStruct((ROWS_P, D_MODEL), F32), jax.ShapeDtypeStruct((ROWS_S, D_MODEL), F32)]
    else:
        out_specs = [pl.BlockSpec((TM, D_MODEL), lambda i: (i, 0))] * 2
        out_shape = [jax.ShapeDtypeStruct((ROWS, D_MODEL), F32), jax.ShapeDtypeStruct((ROWS, D_MODEL), BF16)]
    return pl.pallas_call(
        functools.partial(_out_proj_kernel, len(b_args), len(h_args), final),
        grid=(N_TILES,),
        in_specs=[pl.BlockSpec((TM, half), lambda i: (i, a_col))] + b_specs
                 + [pl.BlockSpec((None, 2 * half, D_MODEL), lambda i: (e, 0, 0))] + h_specs
                 + [pl.BlockSpec((1, D_MODEL), lambda i: (0, 0))],
        out_specs=out_specs,
        out_shape=out_shape,
        compiler_params=_params("arbitrary"),
        name="out_proj",
    )(a, *b_args, w, *h_args, g.reshape(1, D_MODEL))


TN_ODD = 512
N_COL_ODD = W_C // TN_ODD
PRE_SLOTS = BATCH + ROWS_S // TM


def _odd_in_kernel(x_ref, wb_ref, wc_ref, wx_ref, wz_ref, cw_ref, e1_ref, e2_ref, y_ref, pre_ref, buf):
    i = pl.program_id(1)

    @pl.when(i == 0)
    def _():
        buf[0:SUBLANES, :] = jnp.zeros((SUBLANES, TN_ODD), F32)

    x = x_ref[...]
    pre = _dot(x, wc_ref[...]) * _dot(x, wx_ref[...])
    pre_ref[...] = pre
    buf[SUBLANES:SUBLANES + TM, :] = pre
    row = lax.broadcasted_iota(jnp.int32, (TM, TN_ODD), 0)
    t = jnp.where(i < N_TILES_P, (i % TILES_PER_SEQ) * TM + row, row % DEC_SEQ)
    is_sample = i >= N_TILES_P
    pre_m1 = jnp.where(t >= 1, buf[SUBLANES - 1:SUBLANES - 1 + TM, :], jnp.where(is_sample, e1_ref[...], 0.0))
    pre_m2 = jnp.where(t >= 2, buf[SUBLANES - 2:SUBLANES - 2 + TM, :], jnp.where(is_sample, e2_ref[...], 0.0))
    conv = cw_ref[0:1, :] * pre_m2 + cw_ref[1:2, :] * pre_m1 + cw_ref[2:3, :] * pre
    y = _dot(x, wb_ref[...]) * conv * _silu(_dot(x, wz_ref[...]))
    y_ref[...] = y.astype(y_ref.dtype)
    buf[0:SUBLANES, :] = pre[TM - SUBLANES:, :]


def _odd_in(n, w, layer, conv_w, e1, e2):
    wcol = lambda part: (lambda c, i: (layer, 0, part * N_COL_ODD + c))
    e_block = lambda c, i: (jnp.maximum(i - N_TILES_P, 0), c)
    pre_block = lambda c, i: (jnp.where(i < N_TILES_P, i // TILES_PER_SEQ, i - N_TILES_P + BATCH), c)
    return pl.pallas_call(
        _odd_in_kernel,
        grid=(N_COL_ODD, N_TILES),
        in_specs=[pl.BlockSpec((TM, D_MODEL), lambda c, i: (i, 0)),
                  pl.BlockSpec((None, D_MODEL, TN_ODD), wcol(0)),
                  pl.BlockSpec((None, D_MODEL, TN_ODD), wcol(1)),
                  pl.BlockSpec((None, D_MODEL, TN_ODD), wcol(2)),
                  pl.BlockSpec((None, D_MODEL, TN_ODD), wcol(3)),
                  pl.BlockSpec((CONV_W, TN_ODD), lambda c, i: (0, c)),
                  pl.BlockSpec((TM, TN_ODD), e_block),
                  pl.BlockSpec((TM, TN_ODD), e_block)],
        out_specs=[pl.BlockSpec((TM, TN_ODD), lambda c, i: (i, c)),
                   pl.BlockSpec((TM, TN_ODD), pre_block)],
        out_shape=[jax.ShapeDtypeStruct((ROWS, W_C), BF16),
                   jax.ShapeDtypeStruct((PRE_SLOTS * TM, W_C), F32)],
        scratch_shapes=[pltpu.VMEM((SUBLANES + TM, TN_ODD), F32)],
        compiler_params=_params("arbitrary", "arbitrary"),
        name="odd_in",
    )(n, w, w, w, w, conv_w, e1, e2)


def _rope_tables():
    half = HD // 2
    inv = 1.0 / (ROPE_THETA ** (jnp.arange(half, dtype=F32) / half))
    pos_s = PAST_LEN + (jnp.arange(TM_E) % DEC_SEQ)
    pos = jnp.concatenate([jnp.arange(SEQ), pos_s]).astype(F32)
    ang = pos[:, None] * inv[None, :]
    cos = jnp.cos(ang)
    sin = jnp.sin(ang)
    reps = LANES // HD
    return (jnp.tile(jnp.concatenate([cos, cos], axis=1), (1, reps)),
            jnp.tile(jnp.concatenate([-sin, sin], axis=1), (1, reps)))


def _gmlp_weights(w_s, b_s):
    tril = jnp.tril(jnp.ones((CHUNK, CHUNK), bool))
    w_p = jnp.where(tril[None], w_s, 0.0)
    small = w_p[:, :DEC_SEQ, :DEC_SEQ]
    eye = jnp.eye(CHUNK // DEC_SEQ, dtype=F32)
    w_smp = jnp.einsum('ab,hts->hatbs', eye, small).reshape(H_A, CHUNK, CHUNK)
    wm = jnp.stack([w_p, w_smp]).astype(BF16)
    bias_p = jnp.repeat(b_s.T, CH_A, axis=1)
    bias_s = jnp.tile(bias_p[:DEC_SEQ], (CHUNK // DEC_SEQ, 1))
    return wm, jnp.stack([bias_p, bias_s])


def _conv_edges(state):
    e1 = jnp.pad(state[:, CONV_W - 2:], ((0, 0), (0, DEC_SEQ - 1), (0, 0)))
    e2 = jnp.pad(state, ((0, 0), (0, DEC_SEQ - (CONV_W - 1)), (0, 0)))
    return e1.reshape(ROWS_S, W_C), e2.reshape(ROWS_S, W_C)


def _prompt_cache_rows(t_all):
    return jnp.transpose(t_all.reshape(N_EVEN, BATCH, H_B, HD, t_all.shape[-1]), (0, 1, 4, 2, 3))


def kernel(x_prompt, x_sample, cache_b_k, cache_b_v, state_c_conv, norm_w, final_norm_w,
           w_in_even, w_s, b_s, w_out_even, w_in_odd, conv_w, w_out_odd):
    h = (x_prompt.reshape(ROWS_P, D_MODEL), x_sample.reshape(ROWS_S, D_MODEL))
    cos, sin = _rope_tables()
    band_bias = jnp.asarray(_band_bias())
    cc, cn = (jnp.asarray(c) for c in _sample_key_counts())
    kt_all = jnp.transpose(cache_b_k, (0, 1, 3, 4, 2))
    vt_all = jnp.transpose(cache_b_v, (0, 1, 3, 4, 2))
    w_in_even, w_out_even, w_in_odd, w_out_odd = (w.astype(BF16) for w in (w_in_even, w_out_even, w_in_odd, w_out_odd))
    n = _rmsnorm(h, norm_w[0])
    ks_l, vs_l, av_l, cp_l, cs_l = [], [], [], [], []
    new_kt = new_vt = None
    for layer in range(DEPTH):
        last = layer == DEPTH - 1
        g_next = final_norm_w if last else norm_w[layer + 1]
        if layer % 2 == 0:
            e = layer // 2
            wm, bias = _gmlp_weights(w_s[e], b_s[e])
            a_out, av, att_in, kv, new_kt, new_vt = _even_in(n, w_in_even, cos, sin, wm, bias, e, new_kt, new_vt)
            att_s, att_p = _attn(att_in, kt_all, vt_all, e, cc, cn, band_bias)
            h, n = _out_proj(a_out, 0, (att_p, att_s), 0, w_out_even, e, h, g_next, last)
            ks_l.append(kv[:, :W_B].reshape(DEC_BATCH, DEC_SEQ, H_B, HD))
            vs_l.append(kv[:, W_B:].reshape(DEC_BATCH, DEC_SEQ, H_B, HD))
            av_l.append(av)
        else:
            c = layer // 2
            e1, e2 = _conv_edges(state_c_conv[c])
            y, pre = _odd_in(n, w_in_odd, c, conv_w[c], e1, e2)
            h, n = _out_proj(y, 0, y, 1, w_out_odd, c, h, g_next, last)
            pre_p = pre[:BATCH * TM].reshape(BATCH, TM, W_C)
            cp_l.append(pre_p[:, TM - (CONV_W - 1):])
            cs_l.append(pre[BATCH * TM:].reshape(DEC_BATCH, DEC_SEQ, W_C)[:, DEC_SEQ - (CONV_W - 1):])
    y_prompt = h.reshape(BATCH, SEQ, D_MODEL)
    y_sample = n.reshape(DEC_BATCH, DEC_SEQ, D_MODEL)
    new_a_v_sample = jnp.stack(av_l).reshape(N_EVEN, DEC_BATCH, DEC_SEQ, W_A)
    return (y_prompt, y_sample, _prompt_cache_rows(new_kt), _prompt_cache_rows(new_vt), jnp.stack(ks_l), jnp.stack(vs_l),
            new_a_v_sample, jnp.stack(cp_l), jnp.stack(cs_l))
```

```python
import functools

import numpy as np
import jax
import jax.numpy as jnp
from jax import lax
from jax.experimental import pallas as pl
from jax.experimental.pallas import tpu as pltpu

D_MODEL = 1024
BATCH = 4
SEQ = 4096
DEPTH = 4
DEC_BATCH = 128
DEC_SEQ = 8
PAST_LEN = 2048
N_EVEN = (DEPTH + 1) // 2
N_ODD = DEPTH // 2
W_A = D_MODEL
CHUNK = 128
H_A = 8
CH_A = W_A // H_A
W_B = D_MODEL
H_B = 16
HD = W_B // H_B
DILATION_GROUPS = ((128, 1), (512, 4), (2048, 16))
W_MAX = 2048
ROPE_THETA = 10000.0
W_C = 2 * D_MODEL
CONV_W = 3
EPS = 1e-6
NEG = -1e30
EVEN_IN = 3 * W_A + 4 * W_B
ODD_IN = 4 * W_C

LANES = 128
SUBLANES = 8
ROWS_P = BATCH * SEQ
ROWS_S = DEC_BATCH * DEC_SEQ
ROWS = ROWS_P + ROWS_S
TM = 512
N_TILES = ROWS // TM
N_TILES_P = ROWS_P // TM
TILES_PER_SEQ = SEQ // TM
QB = 128
CACHE_LEN = min(W_MAX, PAST_LEN)
HEADS_PER_STEP = 8
N_PAIRS = W_B // LANES
PAIRS_PER_STEP = HEADS_PER_STEP * HD // LANES
Q, K, V, Z = range(4)
VMEM_LIMIT = 48 * 1024 * 1024
ATTN_VMEM_LIMIT = 60 * 1024 * 1024

assert all(window // dil == QB for window, dil in DILATION_GROUPS)

F32 = jnp.float32
BF16 = jnp.bfloat16


def _dot(a, b):
    return jnp.dot(a, b, preferred_element_type=F32)


def _dot_nt(a, b):
    return lax.dot_general(a, b, (((1,), (1,)), ((), ())), preferred_element_type=F32)


def _params(*sem):
    return pltpu.CompilerParams(dimension_semantics=sem, vmem_limit_bytes=VMEM_LIMIT)


def _rmsnorm_kernel(*refs):
    x_refs, (g_ref, o_ref) = refs[:-2], refs[-2:]
    x = _read_tile(x_refs)
    ms = jnp.mean(x * x, axis=-1, keepdims=True)
    o_ref[...] = (x * lax.rsqrt(ms + EPS) * g_ref[...]).astype(o_ref.dtype)


def _rmsnorm(x, g):
    x_specs, x_args = _row_tiles(x, D_MODEL)
    return pl.pallas_call(
        _rmsnorm_kernel,
        grid=(N_TILES,),
        in_specs=x_specs + [pl.BlockSpec((1, D_MODEL), lambda i: (0, 0))],
        out_specs=pl.BlockSpec((TM, D_MODEL), lambda i: (i, 0)),
        out_shape=jax.ShapeDtypeStruct((ROWS, D_MODEL), BF16),
        compiler_params=_params("arbitrary"),
        name="rmsnorm",
    )(*x_args, g.reshape(1, D_MODEL))


def _gelu(x):
    return 0.5 * x * (1.0 + lax.erf(x * np.float32(np.sqrt(0.5))))


def _silu(x):
    return x * jax.nn.sigmoid(x)


TM_E = 256
N_TILES_E = ROWS // TM_E
N_TILES_E_P = ROWS_P // TM_E


TILES_PER_SEQ_E = SEQ // TM_E
KEPT_TILES_E = min(W_MAX, SEQ) // TM_E


def _rope(x, cos, sin, o_ref, kv_ref, kt_s, kind, scale):
    lane = lax.broadcasted_iota(jnp.int32, (TM_E, LANES), 1)
    first_half = (lane % HD) < (HD // 2)
    for p in range(N_PAIRS):
        xc = x[:, p * LANES:(p + 1) * LANES]
        partner = jnp.where(first_half, pltpu.roll(xc, LANES - HD // 2, 1), pltpu.roll(xc, HD // 2, 1))
        out = xc * cos + partner * sin
        if scale is not None:
            out = out * scale
        o_ref[p, kind] = out
        if kind == K:
            kv_ref[:, p * LANES:(p + 1) * LANES] = out
            kt_s[p * LANES:(p + 1) * LANES, :] = out.T


def _even_in_kernel(n_alias, x_ref, w_ref, cos_ref, sin_ref, wm_ref, bias_ref, *refs):
    a_ref, av_ref, o_ref, kv_ref, kt_ref, vt_ref, kt_s, vt_s = refs[n_alias:]
    x = x_ref[...]
    col = lambda j: w_ref[:, j * D_MODEL:(j + 1) * D_MODEL]
    v = _gelu(_dot(x, col(1)))
    av_ref[...] = v
    v_bf = v.astype(BF16)
    u = _gelu(_dot(x, col(0)))
    z = _silu(_dot(x, col(2)))
    for c in range(TM_E // CHUNK):
        rows = slice(c * CHUNK, (c + 1) * CHUNK)
        for h in range(H_A):
            sl = slice(h * CH_A, (h + 1) * CH_A)
            mix = _dot(wm_ref[h], v_bf[rows, sl]) + bias_ref[:, sl]
            a_ref[rows, sl] = (u[rows, sl] * mix * z[rows, sl]).astype(a_ref.dtype)
    cos = cos_ref[...]
    sin = sin_ref[...]
    _rope(_dot(x, col(3)), cos, sin, o_ref, kv_ref, kt_s, Q, np.float32(HD ** -0.5))
    _rope(_dot(x, col(4)), cos, sin, o_ref, kv_ref, kt_s, K, None)
    v_b = _dot(x, col(5))
    kv_ref[:, W_B:2 * W_B] = v_b
    z_b = _silu(_dot(x, col(6)))
    for p in range(N_PAIRS):
        v_p = v_b[:, p * LANES:(p + 1) * LANES]
        o_ref[p, V] = v_p
        vt_s[p * LANES:(p + 1) * LANES, :] = v_p.T
        o_ref[p, Z] = z_b[:, p * LANES:(p + 1) * LANES]

    i = pl.program_id(0)

    @pl.when((i < N_TILES_E_P) & (i % TILES_PER_SEQ_E >= TILES_PER_SEQ_E - KEPT_TILES_E))
    def _():
        kt_ref[...] = kt_s[...]
        vt_ref[...] = vt_s[...]


def _even_in(n, w, cos, sin, wm, bias, e, kt_all, vt_all):
    table = lambda i: (jnp.where(i < N_TILES_E_P, i % TILES_PER_SEQ_E, TILES_PER_SEQ_E), 0)
    variant = lambda i: jnp.where(i < N_TILES_E_P, 0, 1)
    kept_block = lambda i: (e, jnp.minimum(i // TILES_PER_SEQ_E, BATCH - 1), 0,
                            jnp.where(i < N_TILES_E_P,
                                      jnp.maximum(i % TILES_PER_SEQ_E - (TILES_PER_SEQ_E - KEPT_TILES_E), 0),
                                      KEPT_TILES_E - 1))
    t_spec = pl.BlockSpec((None, None, W_B, TM_E), kept_block)
    t_shape = jax.ShapeDtypeStruct((N_EVEN, BATCH, W_B, KEPT_TILES_E * TM_E), F32)
    carried = [] if kt_all is None else [kt_all, vt_all]
    n_in = 6
    return pl.pallas_call(
        functools.partial(_even_in_kernel, len(carried)),
        grid=(N_TILES_E,),
        in_specs=[pl.BlockSpec((TM_E, D_MODEL), lambda i: (i, 0)),
                  pl.BlockSpec((None, D_MODEL, EVEN_IN), lambda i: (e, 0, 0), pipeline_mode=pl.Buffered(1)),
                  pl.BlockSpec((TM_E, LANES), table),
                  pl.BlockSpec((TM_E, LANES), table),
                  pl.BlockSpec((None, H_A, CHUNK, CHUNK), lambda i: (variant(i), 0, 0, 0)),
                  pl.BlockSpec((None, CHUNK, W_A), lambda i: (variant(i), 0, 0))]
                 + [pl.BlockSpec(memory_space=pl.ANY)] * len(carried),
        out_specs=[pl.BlockSpec((TM_E, W_A), lambda i: (i, 0)),
                   pl.BlockSpec((TM_E, W_A), lambda i: (jnp.maximum(i - N_TILES_E_P, 0), 0)),
                   pl.BlockSpec((N_PAIRS, 4, TM_E, LANES), lambda i: (0, 0, i, 0)),
                   pl.BlockSpec((TM_E, 2 * W_B), lambda i: (jnp.maximum(i - N_TILES_E_P, 0), 0)),
                   t_spec, t_spec],
        out_shape=[jax.ShapeDtypeStruct((ROWS, W_A), BF16),
                   jax.ShapeDtypeStruct((ROWS_S, W_A), F32),
                   jax.ShapeDtypeStruct((N_PAIRS, 4, ROWS, LANES), F32),
                   jax.ShapeDtypeStruct((ROWS_S, 2 * W_B), F32),
                   t_shape, t_shape],
        scratch_shapes=[pltpu.VMEM((W_B, TM_E), F32)] * 2,
        input_output_aliases={n_in + j: 4 + j for j in range(len(carried))},
        compiler_params=_params("arbitrary"),
        name="even_in",
    )(n, w, cos, sin, wm, bias, *carried)


def _band_bias():
    i = np.arange(QB)[:, None]
    j = np.arange(2 * QB)[None, :]
    band = (j - i >= 0) & (j - i <= QB)
    first = band & (j >= QB)
    return np.where(np.stack([band, first]), 0.0, NEG).astype(np.float32)


STEPS_PER_UNIT = 8
HALF_BLOCKS = SEQ // QB // 2


def _prompt_attention_step(phase, x_ref, bias_ref, o_ref,
                           q0_s, q1_s, k_s, v_s, o0_s, o1_s, o2_s, lse0_s, lse1_s, lse2_s):
    o_refs = (o0_s, o1_s, o2_s)
    lse_refs = (lse0_s, lse1_s, lse2_s)
    head0 = lax.broadcasted_iota(jnp.int32, (QB, LANES), 1) < HD
    ones = jnp.ones((2 * QB, LANES), BF16)

    @pl.when(phase == 0)
    def _():
        k_s[0:QB, :] = jnp.zeros((QB, LANES), BF16)
        v_s[0:QB, :] = jnp.zeros((QB, LANES), BF16)

    for g, (_, dil) in enumerate(DILATION_GROUPS):
        n = SEQ // dil
        ch = min(n, 4 * QB)
        n_ch = n // ch
        n_blk = n // QB

        def deinterleave(idx, carry, dil=dil, ch=ch, n_ch=n_ch):
            start = idx // n_ch + dil * ch * (idx % n_ch)
            src = pl.ds(pl.multiple_of(start, ch), ch) if dil == 1 else pl.ds(start, ch, stride=dil)
            dst = pl.ds(pl.multiple_of(idx * ch, QB), ch)
            dst_pad = pl.ds(pl.multiple_of(QB + idx * ch, QB), ch)
            first = lax.broadcasted_iota(jnp.int32, (ch, LANES), 1) < HD
            q = x_ref[Q, src, :]
            q0_s[dst, :] = jnp.where(first, q, 0.0).astype(BF16)
            q1_s[dst, :] = jnp.where(first, 0.0, q).astype(BF16)
            k_s[dst_pad, :] = x_ref[K, src, :].astype(BF16)
            v_s[dst_pad, :] = x_ref[V, src, :].astype(BF16)
            return carry

        @pl.when(phase == 2 * g)
        def _(ch=ch, deinterleave=deinterleave):
            lax.fori_loop(0, SEQ // ch, deinterleave, 0)

        def body(i, carry, g=g, dil=dil, n_blk=n_blk):
            blk = (phase % 2) * HALF_BLOCKS + i
            res = blk // n_blk
            pos = blk % n_blk
            rows = pl.ds(pl.multiple_of(blk * QB, QB), QB)
            rows2 = pl.ds(pl.multiple_of(blk * QB, QB), 2 * QB)
            k2 = k_s[rows2, :]
            v_aug = jnp.concatenate([v_s[rows2, :], ones], axis=1)
            bias = bias_ref[jnp.where(pos > 0, 0, 1)]
            q2 = jnp.concatenate([q0_s[rows, :], q1_s[rows, :]], axis=0)
            s = _dot_nt(q2, k2) + jnp.concatenate([bias, bias], axis=0)
            m = jnp.max(s, axis=1, keepdims=True)
            r = _dot(jnp.exp(s - m).astype(BF16), v_aug)
            r = jnp.where(jnp.concatenate([head0, head0], axis=1), r[:QB], r[QB:])
            l = r[:, LANES:]
            start = res + dil * QB * pos
            dst = pl.ds(pl.multiple_of(start, QB), QB) if dil == 1 else pl.ds(start, QB, stride=dil)
            o_refs[g][dst, :] = r[:, :LANES] / l
            lse_refs[g][dst, :] = jnp.where(head0, m[:QB], m[QB:]) + jnp.log(l)
            return carry

        @pl.when(phase // 2 == g)
        def _(body=body):
            lax.fori_loop(0, HALF_BLOCKS, body, 0, unroll=16)

    def finish(i, carry):
        blk = (phase % 2) * HALF_BLOCKS + i
        rows = pl.ds(pl.multiple_of(blk * QB, QB), QB)
        lses = [ref[rows, :] for ref in lse_refs]
        top = jnp.maximum(jnp.maximum(lses[0], lses[1]), lses[2])
        wts = [jnp.exp(lse - top) for lse in lses]
        num = wts[0] * o0_s[rows, :] + wts[1] * o1_s[rows, :] + wts[2] * o2_s[rows, :]
        den = wts[0] + wts[1] + wts[2]
        o_ref[rows, :] = (num / den * x_ref[Z, rows, :]).astype(o_ref.dtype)
        return carry

    @pl.when(phase // 2 == len(DILATION_GROUPS))
    def _():
        lax.fori_loop(0, HALF_BLOCKS, finish, 0)


def _sample_key_counts():
    i = (np.arange(HEADS_PER_STEP * DEC_SEQ) % DEC_SEQ)[:, None]

    def count(dist):
        total = np.zeros(dist.shape, np.float32)
        for window, dil in DILATION_GROUPS:
            total += (dist >= 0) & (dist <= window) & (dist % dil == 0)
        return total

    cache = count(CACHE_LEN + i - np.arange(CACHE_LEN)[None, :])
    new = count(i - np.arange(QB)[None, :]) * (np.arange(QB)[None, :] < DEC_SEQ)
    return cache, new.astype(np.float32)


def _sample_attention_step(x_ref, kt_ref, vt_ref, cc_ref, cn_ref, o_ref):
    width = HEADS_PER_STEP * HD
    n_q = HEADS_PER_STEP * DEC_SEQ
    wide = lambda kind: jnp.concatenate([x_ref[p, kind] for p in range(PAIRS_PER_STEP)], axis=1)
    lane = lax.broadcasted_iota(jnp.int32, (n_q, width), 1)
    row = lax.broadcasted_iota(jnp.int32, (n_q, width), 0)
    own_head = (lane // HD) == (row // DEC_SEQ)
    q_rep = jnp.concatenate([wide(Q)] * HEADS_PER_STEP, axis=0)
    q_big = jnp.where(own_head, q_rep, 0.0).astype(BF16)
    pad = jnp.zeros((QB - DEC_SEQ, width), F32)
    kn = jnp.concatenate([wide(K), pad], axis=0).astype(BF16)
    vn = jnp.concatenate([wide(V), pad], axis=0).astype(BF16)
    cc = cc_ref[...]
    cn = cn_ref[...]
    kt = kt_ref[...].reshape(width, CACHE_LEN).astype(BF16)
    s_c = jnp.where(cc > 0.0, _dot(q_big, kt), NEG)
    s_n = jnp.where(cn > 0.0, _dot_nt(q_big, kn), NEG)
    m = jnp.maximum(jnp.max(s_c, axis=1, keepdims=True), jnp.max(s_n, axis=1, keepdims=True))
    e_c = jnp.exp(s_c - m) * cc
    e_n = jnp.exp(s_n - m) * cn
    l = jnp.sum(e_c, axis=1, keepdims=True) + jnp.sum(e_n, axis=1, keepdims=True)
    vt = vt_ref[...].reshape(width, CACHE_LEN).astype(BF16)
    out_big = (_dot_nt(e_c.astype(BF16), vt) + _dot(e_n.astype(BF16), vn)) / l
    out_big = jnp.where(own_head, out_big, 0.0)
    out = jnp.sum(out_big.reshape(HEADS_PER_STEP, DEC_SEQ, width), axis=0)
    o_ref[...] = (out * wide(Z)).astype(o_ref.dtype)


def _attn_kernel(xs_ref, kt_ref, vt_ref, cc_ref, cn_ref, xp_ref, bias_ref, os_ref, op_ref, *scratch):
    _sample_attention_step(xs_ref, kt_ref, vt_ref, cc_ref, cn_ref, os_ref)
    _prompt_attention_step(pl.program_id(0) % STEPS_PER_UNIT, xp_ref, bias_ref, op_ref, *scratch)


def _attn(x, kt_all, vt_all, e, cc, cn, bias):
    first = ROWS_P // DEC_SEQ
    width = HEADS_PER_STEP * HD
    per_row = H_B // HEADS_PER_STEP
    n_steps = DEC_BATCH * per_row
    assert n_steps == BATCH * N_PAIRS * STEPS_PER_UNIT
    n_q = HEADS_PER_STEP * DEC_SEQ
    once = pl.Buffered(1)
    unit = lambda s: s // STEPS_PER_UNIT
    cache_spec = pl.BlockSpec((None, None, HEADS_PER_STEP, HD, CACHE_LEN),
                              lambda s: (e, s // per_row, s % per_row, 0, 0))
    return pl.pallas_call(
        _attn_kernel,
        grid=(n_steps,),
        in_specs=[pl.BlockSpec((PAIRS_PER_STEP, 4, DEC_SEQ, LANES), lambda s: (s % per_row, 0, first + s // per_row, 0)),
                  cache_spec, cache_spec,
                  pl.BlockSpec((n_q, CACHE_LEN), lambda s: (0, 0), pipeline_mode=once),
                  pl.BlockSpec((n_q, QB), lambda s: (0, 0), pipeline_mode=once),
                  pl.BlockSpec((None, 4, SEQ, LANES), lambda s: (unit(s) % N_PAIRS, 0, unit(s) // N_PAIRS, 0)),
                  pl.BlockSpec((2, QB, 2 * QB), lambda s: (0, 0, 0), pipeline_mode=once)],
        out_specs=[pl.BlockSpec((DEC_SEQ, width), lambda s: (s // per_row, s % per_row)),
                   pl.BlockSpec((SEQ, LANES), lambda s: (unit(s) // N_PAIRS, unit(s) % N_PAIRS))],
        out_shape=[jax.ShapeDtypeStruct((ROWS_S, W_B), BF16),
                   jax.ShapeDtypeStruct((ROWS_P, W_B), BF16)],
        scratch_shapes=[pltpu.VMEM((SEQ, LANES), BF16)] * 2
                       + [pltpu.VMEM((QB + SEQ, LANES), BF16)] * 2
                       + [pltpu.VMEM((SEQ, LANES), F32)] * 6,
        compiler_params=pltpu.CompilerParams(dimension_semantics=("arbitrary",),
                                             vmem_limit_bytes=ATTN_VMEM_LIMIT),
        name="attn",
    )(x, kt_all, vt_all, cc, cn, x, bias)


def _row_tiles(x, width, col=0):
    if not isinstance(x, tuple):
        return [pl.BlockSpec((TM, width), lambda i: (i, col))], [x]
    x_p, x_s = x
    return ([pl.BlockSpec((TM, width), lambda i: (jnp.minimum(i, N_TILES_P - 1), col)),
             pl.BlockSpec((TM, width), lambda i: (jnp.maximum(i - N_TILES_P, 0), col))],
            [x_p, x_s])


def _read_tile(refs):
    if len(refs) == 1:
        return refs[0][...]
    return jnp.where(pl.program_id(0) < N_TILES_P, refs[0][...], refs[1][...])


def _out_proj_kernel(n_b, n_h, final, a_ref, *refs):
    b_refs, refs = refs[:n_b], refs[n_b:]
    w_ref, refs = refs[0], refs[1:]
    h_refs, (g_ref, *out_refs) = refs[:n_h], refs[n_h:]
    half = w_ref.shape[0] // 2
    h_new = (_read_tile(h_refs) + _dot(a_ref[...], w_ref[0:half, :])
             + _dot(_read_tile(b_refs), w_ref[half:, :]))
    ms = jnp.mean(h_new * h_new, axis=-1, keepdims=True)
    normed = h_new * lax.rsqrt(ms + EPS) * g_ref[...]
    if not final:
        h_out_ref, n_out_ref = out_refs
        h_out_ref[...] = h_new
        n_out_ref[...] = normed.astype(n_out_ref.dtype)
        return
    y_p_ref, y_s_ref = out_refs
    y_s_ref[...] = normed

    @pl.when(pl.program_id(0) < N_TILES_P)
    def _():
        y_p_ref[...] = normed


def _out_proj(a, a_col, b, b_col, w, e, h, g, final):
    half = w.shape[1] // 2
    b_specs, b_args = _row_tiles(b, half, b_col)
    h_specs, h_args = _row_tiles(h, D_MODEL)
    if final:
        out_specs = [pl.BlockSpec((TM, D_MODEL), lambda i: (jnp.minimum(i, N_TILES_P - 1), 0)),
                     pl.BlockSpec((TM, D_MODEL), lambda i: (jnp.maximum(i - N_TILES_P, 0), 0))]
        out_shape = [jax.ShapeDtypeStruct((ROWS_P, D_MODEL), F32), jax.ShapeDtypeStruct((ROWS_S, D_MODEL), F32)]
    else:
        out_specs = [pl.BlockSpec((TM, D_MODEL), lambda i: (i, 0))] * 2
        out_shape = [jax.ShapeDtypeStruct((ROWS, D_MODEL), F32), jax.ShapeDtypeStruct((ROWS, D_MODEL), BF16)]
    return pl.pallas_call(
        functools.partial(_out_proj_kernel, len(b_args), len(h_args), final),
        grid=(N_TILES,),
        in_specs=[pl.BlockSpec((TM, half), lambda i: (i, a_col))] + b_specs
                 + [pl.BlockSpec((None, 2 * half, D_MODEL), lambda i: (e, 0, 0))] + h_specs
                 + [pl.BlockSpec((1, D_MODEL), lambda i: (0, 0))],
        out_specs=out_specs,
        out_shape=out_shape,
        compiler_params=_params("arbitrary"),
        name="out_proj",
    )(a, *b_args, w, *h_args, g.reshape(1, D_MODEL))


TM_O = 256
N_TILES_O = ROWS // TM_O
N_TILES_O_P = ROWS_P // TM_O
TILES_PER_SEQ_O = SEQ // TM_O
TN_ODD = 512
PRE_SLOTS = BATCH + ROWS_S // TM_O


def _odd_in_kernel(x_ref, w_ref, cw_ref, e1_ref, e2_ref, y_ref, pre_ref, buf):
    i = pl.program_id(0)

    @pl.when(i == 0)
    def _():
        buf[0:SUBLANES, :] = jnp.zeros((SUBLANES, W_C), F32)

    x = x_ref[...]
    row = lax.broadcasted_iota(jnp.int32, (TM_O, TN_ODD), 0)
    t = jnp.where(i < N_TILES_O_P, (i % TILES_PER_SEQ_O) * TM_O + row, row % DEC_SEQ)
    is_sample = i >= N_TILES_O_P
    for c in range(W_C // TN_ODD):
        cols = slice(c * TN_ODD, (c + 1) * TN_ODD)
        part = lambda k: w_ref[:, k * W_C + c * TN_ODD:k * W_C + (c + 1) * TN_ODD]
        pre = _dot(x, part(1)) * _dot(x, part(2))
        pre_ref[:, cols] = pre
        buf[SUBLANES:SUBLANES + TM_O, cols] = pre
        pre_m1 = jnp.where(t >= 1, buf[SUBLANES - 1:SUBLANES - 1 + TM_O, cols],
                           jnp.where(is_sample, e1_ref[:, cols], 0.0))
        pre_m2 = jnp.where(t >= 2, buf[SUBLANES - 2:SUBLANES - 2 + TM_O, cols],
                           jnp.where(is_sample, e2_ref[:, cols], 0.0))
        conv = cw_ref[0:1, cols] * pre_m2 + cw_ref[1:2, cols] * pre_m1 + cw_ref[2:3, cols] * pre
        y = _dot(x, part(0)) * conv * _silu(_dot(x, part(3)))
        y_ref[:, cols] = y.astype(y_ref.dtype)
        buf[0:SUBLANES, cols] = pre[TM_O - SUBLANES:, :]


def _odd_in(n, w, layer, conv_w, e1, e2):
    e_block = lambda i: (jnp.maximum(i - N_TILES_O_P, 0), 0)
    pre_block = lambda i: (jnp.where(i < N_TILES_O_P, i // TILES_PER_SEQ_O, i - N_TILES_O_P + BATCH), 0)
    return pl.pallas_call(
        _odd_in_kernel,
        grid=(N_TILES_O,),
        in_specs=[pl.BlockSpec((TM_O, D_MODEL), lambda i: (i, 0)),
                  pl.BlockSpec((None, D_MODEL, ODD_IN), lambda i: (layer, 0, 0), pipeline_mode=pl.Buffered(1)),
                  pl.BlockSpec((CONV_W, W_C), lambda i: (0, 0)),
                  pl.BlockSpec((TM_O, W_C), e_block),
                  pl.BlockSpec((TM_O, W_C), e_block)],
        out_specs=[pl.BlockSpec((TM_O, W_C), lambda i: (i, 0)),
                   pl.BlockSpec((TM_O, W_C), pre_block)],
        out_shape=[jax.ShapeDtypeStruct((ROWS, W_C), BF16),
                   jax.ShapeDtypeStruct((PRE_SLOTS * TM_O, W_C), F32)],
        scratch_shapes=[pltpu.VMEM((SUBLANES + TM_O, W_C), F32)],
        compiler_params=_params("arbitrary"),
        name="odd_in",
    )(n, w, conv_w, e1, e2)


def _rope_tables():
    half = HD // 2
    inv = 1.0 / (ROPE_THETA ** (jnp.arange(half, dtype=F32) / half))
    pos_s = PAST_LEN + (jnp.arange(TM_E) % DEC_SEQ)
    pos = jnp.concatenate([jnp.arange(SEQ), pos_s]).astype(F32)
    ang = pos[:, None] * inv[None, :]
    cos = jnp.cos(ang)
    sin = jnp.sin(ang)
    reps = LANES // HD
    return (jnp.tile(jnp.concatenate([cos, cos], axis=1), (1, reps)),
            jnp.tile(jnp.concatenate([-sin, sin], axis=1), (1, reps)))


def _gmlp_weights(w_s, b_s):
    tril = jnp.tril(jnp.ones((CHUNK, CHUNK), bool))
    w_p = jnp.where(tril[None], w_s, 0.0)
    small = w_p[:, :DEC_SEQ, :DEC_SEQ]
    eye = jnp.eye(CHUNK // DEC_SEQ, dtype=F32)
    w_smp = jnp.einsum('ab,hts->hatbs', eye, small).reshape(H_A, CHUNK, CHUNK)
    wm = jnp.stack([w_p, w_smp]).astype(BF16)
    bias_p = jnp.repeat(b_s.T, CH_A, axis=1)
    bias_s = jnp.tile(bias_p[:DEC_SEQ], (CHUNK // DEC_SEQ, 1))
    return wm, jnp.stack([bias_p, bias_s])


def _conv_edges(state):
    e1 = jnp.pad(state[:, CONV_W - 2:], ((0, 0), (0, DEC_SEQ - 1), (0, 0)))
    e2 = jnp.pad(state, ((0, 0), (0, DEC_SEQ - (CONV_W - 1)), (0, 0)))
    return e1.reshape(ROWS_S, W_C), e2.reshape(ROWS_S, W_C)


def _prompt_cache_rows(t_all):
    return jnp.transpose(t_all.reshape(N_EVEN, BATCH, H_B, HD, t_all.shape[-1]), (0, 1, 4, 2, 3))


def kernel(x_prompt, x_sample, cache_b_k, cache_b_v, state_c_conv, norm_w, final_norm_w,
           w_in_even, w_s, b_s, w_out_even, w_in_odd, conv_w, w_out_odd):
    h = (x_prompt.reshape(ROWS_P, D_MODEL), x_sample.reshape(ROWS_S, D_MODEL))
    cos, sin = _rope_tables()
    band_bias = jnp.asarray(_band_bias())
    cc, cn = (jnp.asarray(c) for c in _sample_key_counts())
    kt_all = jnp.transpose(cache_b_k, (0, 1, 3, 4, 2))
    vt_all = jnp.transpose(cache_b_v, (0, 1, 3, 4, 2))
    w_in_even, w_out_even, w_in_odd, w_out_odd = (w.astype(BF16) for w in (w_in_even, w_out_even, w_in_odd, w_out_odd))
    n = _rmsnorm(h, norm_w[0])
    ks_l, vs_l, av_l, cp_l, cs_l = [], [], [], [], []
    new_kt = new_vt = None
    for layer in range(DEPTH):
        last = layer == DEPTH - 1
        g_next = final_norm_w if last else norm_w[layer + 1]
        if layer % 2 == 0:
            e = layer // 2
            wm, bias = _gmlp_weights(w_s[e], b_s[e])
            a_out, av, att_in, kv, new_kt, new_vt = _even_in(n, w_in_even, cos, sin, wm, bias, e, new_kt, new_vt)
            att_s, att_p = _attn(att_in, kt_all, vt_all, e, cc, cn, band_bias)
            h, n = _out_proj(a_out, 0, (att_p, att_s), 0, w_out_even, e, h, g_next, last)
            ks_l.append(kv[:, :W_B].reshape(DEC_BATCH, DEC_SEQ, H_B, HD))
            vs_l.append(kv[:, W_B:].reshape(DEC_BATCH, DEC_SEQ, H_B, HD))
            av_l.append(av)
        else:
            c = layer // 2
            e1, e2 = _conv_edges(state_c_conv[c])
            y, pre = _odd_in(n, w_in_odd, c, conv_w[c], e1, e2)
            h, n = _out_proj(y, 0, y, 1, w_out_odd, c, h, g_next, last)
            pre_p = pre[:BATCH * TM_O].reshape(BATCH, TM_O, W_C)
            cp_l.append(pre_p[:, TM_O - (CONV_W - 1):])
            cs_l.append(pre[BATCH * TM_O:].reshape(DEC_BATCH, DEC_SEQ, W_C)[:, DEC_SEQ - (CONV_W - 1):])
    y_prompt = h.reshape(BATCH, SEQ, D_MODEL)
    y_sample = n.reshape(DEC_BATCH, DEC_SEQ, D_MODEL)
    new_a_v_sample = jnp.stack(av_l).reshape(N_EVEN, DEC_BATCH, DEC_SEQ, W_A)
    return (y_prompt, y_sample, _prompt_cache_rows(new_kt), _prompt_cache_rows(new_vt), jnp.stack(ks_l), jnp.stack(vs_l),
            new_a_v_sample, jnp.stack(cp_l), jnp.stack(cs_l))
```

```python
import functools

import numpy as np
import jax
import jax.numpy as jnp
from jax import lax
from jax.experimental import pallas as pl
from jax.experimental.pallas import tpu as pltpu

D_MODEL = 1024
BATCH = 4
SEQ = 4096
DEPTH = 4
DEC_BATCH = 128
DEC_SEQ = 8
PAST_LEN = 2048
N_EVEN = (DEPTH + 1) // 2
N_ODD = DEPTH // 2
W_A = D_MODEL
CHUNK = 128
H_A = 8
CH_A = W_A // H_A
W_B = D_MODEL
H_B = 16
HD = W_B // H_B
DILATION_GROUPS = ((128, 1), (512, 4), (2048, 16))
W_MAX = 2048
ROPE_THETA = 10000.0
W_C = 2 * D_MODEL
CONV_W = 3
EPS = 1e-6
NEG = -1e30
EVEN_IN = 3 * W_A + 4 * W_B
ODD_IN = 4 * W_C

LANES = 128
SUBLANES = 8
ROWS_P = BATCH * SEQ
ROWS_S = DEC_BATCH * DEC_SEQ
ROWS = ROWS_P + ROWS_S
TM = 512
N_TILES = ROWS // TM
N_TILES_P = ROWS_P // TM
TILES_PER_SEQ = SEQ // TM
QB = 128
CACHE_LEN = min(W_MAX, PAST_LEN)
HEADS_PER_STEP = 8
N_PAIRS = W_B // LANES
PAIRS_PER_STEP = HEADS_PER_STEP * HD // LANES
Q, K, V, Z = range(4)
VMEM_LIMIT = 48 * 1024 * 1024
ATTN_VMEM_LIMIT = 60 * 1024 * 1024

assert all(window // dil == QB for window, dil in DILATION_GROUPS)

F32 = jnp.float32
BF16 = jnp.bfloat16


def _dot(a, b):
    return jnp.dot(a, b, preferred_element_type=F32)


def _dot_nt(a, b):
    return lax.dot_general(a, b, (((1,), (1,)), ((), ())), preferred_element_type=F32)


def _params(*sem):
    return pltpu.CompilerParams(dimension_semantics=sem, vmem_limit_bytes=VMEM_LIMIT)


def _rmsnorm_kernel(*refs):
    x_refs, (g_ref, o_ref) = refs[:-2], refs[-2:]
    x = _read_tile(x_refs)
    ms = jnp.mean(x * x, axis=-1, keepdims=True)
    o_ref[...] = (x * lax.rsqrt(ms + EPS) * g_ref[...]).astype(o_ref.dtype)


def _rmsnorm(x, g):
    x_specs, x_args = _row_tiles(x, D_MODEL)
    return pl.pallas_call(
        _rmsnorm_kernel,
        grid=(N_TILES,),
        in_specs=x_specs + [pl.BlockSpec((1, D_MODEL), lambda i: (0, 0))],
        out_specs=pl.BlockSpec((TM, D_MODEL), lambda i: (i, 0)),
        out_shape=jax.ShapeDtypeStruct((ROWS, D_MODEL), BF16),
        compiler_params=_params("arbitrary"),
        name="rmsnorm",
    )(*x_args, g.reshape(1, D_MODEL))


def _gelu(x):
    return 0.5 * x * (1.0 + lax.erf(x * np.float32(np.sqrt(0.5))))


def _silu(x):
    return x * jax.nn.sigmoid(x)


TM_E = 256
N_TILES_E = ROWS // TM_E
N_TILES_E_P = ROWS_P // TM_E


TILES_PER_SEQ_E = SEQ // TM_E
KEPT_TILES_E = min(W_MAX, SEQ) // TM_E


def _rope(x, cos, sin, o_ref, kv_ref, kt_s, kind, scale):
    lane = lax.broadcasted_iota(jnp.int32, (TM_E, LANES), 1)
    first_half = (lane % HD) < (HD // 2)
    for p in range(N_PAIRS):
        xc = x[:, p * LANES:(p + 1) * LANES]
        partner = jnp.where(first_half, pltpu.roll(xc, LANES - HD // 2, 1), pltpu.roll(xc, HD // 2, 1))
        out = xc * cos + partner * sin
        if scale is not None:
            out = out * scale
        o_ref[p, kind] = out
        if kind == K:
            kv_ref[:, p * LANES:(p + 1) * LANES] = out
            kt_s[p * LANES:(p + 1) * LANES, :] = out.T


def _even_in_kernel(n_alias, x_ref, w_ref, cos_ref, sin_ref, wm_ref, bias_ref, *refs):
    a_ref, av_ref, o_ref, kv_ref, kt_ref, vt_ref, kt_s, vt_s = refs[n_alias:]
    x = x_ref[...]
    col = lambda j: w_ref[:, j * D_MODEL:(j + 1) * D_MODEL]
    v = _gelu(_dot(x, col(1)))
    av_ref[...] = v
    v_bf = v.astype(BF16)
    u = _gelu(_dot(x, col(0)))
    z = _silu(_dot(x, col(2)))
    for c in range(TM_E // CHUNK):
        rows = slice(c * CHUNK, (c + 1) * CHUNK)
        for h in range(H_A):
            sl = slice(h * CH_A, (h + 1) * CH_A)
            mix = _dot(wm_ref[h], v_bf[rows, sl]) + bias_ref[:, sl]
            a_ref[rows, sl] = (u[rows, sl] * mix * z[rows, sl]).astype(a_ref.dtype)
    cos = cos_ref[...]
    sin = sin_ref[...]
    _rope(_dot(x, col(3)), cos, sin, o_ref, kv_ref, kt_s, Q, np.float32(HD ** -0.5))
    _rope(_dot(x, col(4)), cos, sin, o_ref, kv_ref, kt_s, K, None)
    v_b = _dot(x, col(5))
    kv_ref[:, W_B:2 * W_B] = v_b
    z_b = _silu(_dot(x, col(6)))
    for p in range(N_PAIRS):
        v_p = v_b[:, p * LANES:(p + 1) * LANES]
        o_ref[p, V] = v_p
        vt_s[p * LANES:(p + 1) * LANES, :] = v_p.T
        o_ref[p, Z] = z_b[:, p * LANES:(p + 1) * LANES]

    i = pl.program_id(0)

    @pl.when((i < N_TILES_E_P) & (i % TILES_PER_SEQ_E >= TILES_PER_SEQ_E - KEPT_TILES_E))
    def _():
        kt_ref[...] = kt_s[...]
        vt_ref[...] = vt_s[...]


def _even_in(n, w, cos, sin, wm, bias, e, kt_all, vt_all):
    table = lambda i: (jnp.where(i < N_TILES_E_P, i % TILES_PER_SEQ_E, TILES_PER_SEQ_E), 0)
    variant = lambda i: jnp.where(i < N_TILES_E_P, 0, 1)
    kept_block = lambda i: (e, jnp.minimum(i // TILES_PER_SEQ_E, BATCH - 1), 0,
                            jnp.where(i < N_TILES_E_P,
                                      jnp.maximum(i % TILES_PER_SEQ_E - (TILES_PER_SEQ_E - KEPT_TILES_E), 0),
                                      KEPT_TILES_E - 1))
    t_spec = pl.BlockSpec((None, W_B, TM_E), lambda i: kept_block(i)[1:])
    t_shape = jax.ShapeDtypeStruct((BATCH, W_B, KEPT_TILES_E * TM_E), F32)
    carried = [] if kt_all is None else [kt_all, vt_all]
    n_in = 6
    return pl.pallas_call(
        functools.partial(_even_in_kernel, len(carried)),
        grid=(N_TILES_E,),
        in_specs=[pl.BlockSpec((TM_E, D_MODEL), lambda i: (i, 0)),
                  pl.BlockSpec((None, D_MODEL, EVEN_IN), lambda i: (e, 0, 0), pipeline_mode=pl.Buffered(1)),
                  pl.BlockSpec((TM_E, LANES), table),
                  pl.BlockSpec((TM_E, LANES), table),
                  pl.BlockSpec((None, H_A, CHUNK, CHUNK), lambda i: (variant(i), 0, 0, 0)),
                  pl.BlockSpec((None, CHUNK, W_A), lambda i: (variant(i), 0, 0))]
                 + [pl.BlockSpec(memory_space=pl.ANY)] * len(carried),
        out_specs=[pl.BlockSpec((TM_E, W_A), lambda i: (i, 0)),
                   pl.BlockSpec((TM_E, W_A), lambda i: (jnp.maximum(i - N_TILES_E_P, 0), 0)),
                   pl.BlockSpec((N_PAIRS, 4, TM_E, LANES), lambda i: (0, 0, i, 0)),
                   pl.BlockSpec((TM_E, 2 * W_B), lambda i: (jnp.maximum(i - N_TILES_E_P, 0), 0)),
                   t_spec, t_spec],
        out_shape=[jax.ShapeDtypeStruct((ROWS, W_A), BF16),
                   jax.ShapeDtypeStruct((ROWS_S, W_A), F32),
                   jax.ShapeDtypeStruct((N_PAIRS, 4, ROWS, LANES), F32),
                   jax.ShapeDtypeStruct((ROWS_S, 2 * W_B), F32),
                   t_shape, t_shape],
        scratch_shapes=[pltpu.VMEM((W_B, TM_E), F32)] * 2,
        input_output_aliases={n_in + j: 4 + j for j in range(len(carried))},
        compiler_params=_params("arbitrary"),
        name="even_in",
    )(n, w, cos, sin, wm, bias, *carried)


def _band_bias():
    i = np.arange(QB)[:, None]
    j = np.arange(2 * QB)[None, :]
    band = (j - i >= 0) & (j - i <= QB)
    first = band & (j >= QB)
    return np.where(np.stack([band, first]), 0.0, NEG).astype(np.float32)


STEPS_PER_UNIT = 8
HALF_BLOCKS = SEQ // QB // 2


def _prompt_attention_step(phase, x_ref, bias_ref, o_ref,
                           q0_s, q1_s, k_s, v_s, o0_s, o1_s, o2_s, lse0_s, lse1_s, lse2_s):
    o_refs = (o0_s, o1_s, o2_s)
    lse_refs = (lse0_s, lse1_s, lse2_s)
    head0 = lax.broadcasted_iota(jnp.int32, (QB, LANES), 1) < HD
    ones = jnp.ones((2 * QB, LANES), BF16)

    @pl.when(phase == 0)
    def _():
        k_s[0:QB, :] = jnp.zeros((QB, LANES), BF16)
        v_s[0:QB, :] = jnp.zeros((QB, LANES), BF16)

    for g, (_, dil) in enumerate(DILATION_GROUPS):
        n = SEQ // dil
        ch = min(n, 4 * QB)
        n_ch = n // ch
        n_blk = n // QB

        def deinterleave(idx, carry, dil=dil, ch=ch, n_ch=n_ch):
            start = idx // n_ch + dil * ch * (idx % n_ch)
            src = pl.ds(pl.multiple_of(start, ch), ch) if dil == 1 else pl.ds(start, ch, stride=dil)
            dst = pl.ds(pl.multiple_of(idx * ch, QB), ch)
            dst_pad = pl.ds(pl.multiple_of(QB + idx * ch, QB), ch)
            first = lax.broadcasted_iota(jnp.int32, (ch, LANES), 1) < HD
            q = x_ref[Q, src, :]
            q0_s[dst, :] = jnp.where(first, q, 0.0).astype(BF16)
            q1_s[dst, :] = jnp.where(first, 0.0, q).astype(BF16)
            k_s[dst_pad, :] = x_ref[K, src, :].astype(BF16)
            v_s[dst_pad, :] = x_ref[V, src, :].astype(BF16)
            return carry

        @pl.when(phase == 2 * g)
        def _(ch=ch, deinterleave=deinterleave):
            lax.fori_loop(0, SEQ // ch, deinterleave, 0)

        def body(i, carry, g=g, dil=dil, n_blk=n_blk):
            blk = (phase % 2) * HALF_BLOCKS + i
            res = blk // n_blk
            pos = blk % n_blk
            rows = pl.ds(pl.multiple_of(blk * QB, QB), QB)
            rows2 = pl.ds(pl.multiple_of(blk * QB, QB), 2 * QB)
            k2 = k_s[rows2, :]
            v_aug = jnp.concatenate([v_s[rows2, :], ones], axis=1)
            bias = bias_ref[jnp.where(pos > 0, 0, 1)]
            q2 = jnp.concatenate([q0_s[rows, :], q1_s[rows, :]], axis=0)
            s = _dot_nt(q2, k2) + jnp.concatenate([bias, bias], axis=0)
            m = jnp.max(s, axis=1, keepdims=True)
            r = _dot(jnp.exp(s - m).astype(BF16), v_aug)
            r = jnp.where(jnp.concatenate([head0, head0], axis=1), r[:QB], r[QB:])
            l = r[:, LANES:]
            start = res + dil * QB * pos
            dst = pl.ds(pl.multiple_of(start, QB), QB) if dil == 1 else pl.ds(start, QB, stride=dil)
            o_refs[g][dst, :] = r[:, :LANES] / l
            lse_refs[g][dst, :] = jnp.where(head0, m[:QB], m[QB:]) + jnp.log(l)
            return carry

        @pl.when(phase // 2 == g)
        def _(body=body):
            lax.fori_loop(0, HALF_BLOCKS, body, 0, unroll=16)

    def finish(i, carry):
        blk = (phase % 2) * HALF_BLOCKS + i
        rows = pl.ds(pl.multiple_of(blk * QB, QB), QB)
        lses = [ref[rows, :] for ref in lse_refs]
        top = jnp.maximum(jnp.maximum(lses[0], lses[1]), lses[2])
        wts = [jnp.exp(lse - top) for lse in lses]
        num = wts[0] * o0_s[rows, :] + wts[1] * o1_s[rows, :] + wts[2] * o2_s[rows, :]
        den = wts[0] + wts[1] + wts[2]
        o_ref[rows, :] = (num / den * x_ref[Z, rows, :]).astype(o_ref.dtype)
        return carry

    @pl.when(phase // 2 == len(DILATION_GROUPS))
    def _():
        lax.fori_loop(0, HALF_BLOCKS, finish, 0)


def _sample_key_counts():
    i = (np.arange(HEADS_PER_STEP * DEC_SEQ) % DEC_SEQ)[:, None]

    def count(dist):
        total = np.zeros(dist.shape, np.float32)
        for window, dil in DILATION_GROUPS:
            total += (dist >= 0) & (dist <= window) & (dist % dil == 0)
        return total

    cache = count(CACHE_LEN + i - np.arange(CACHE_LEN)[None, :])
    new = count(i - np.arange(QB)[None, :]) * (np.arange(QB)[None, :] < DEC_SEQ)
    return cache, new.astype(np.float32)


def _sample_attention_step(x_ref, kt_ref, vt_ref, cc_ref, cn_ref, o_ref):
    width = HEADS_PER_STEP * HD
    n_q = HEADS_PER_STEP * DEC_SEQ
    wide = lambda kind: jnp.concatenate([x_ref[p, kind] for p in range(PAIRS_PER_STEP)], axis=1)
    lane = lax.broadcasted_iota(jnp.int32, (n_q, width), 1)
    row = lax.broadcasted_iota(jnp.int32, (n_q, width), 0)
    own_head = (lane // HD) == (row // DEC_SEQ)
    q_rep = jnp.concatenate([wide(Q)] * HEADS_PER_STEP, axis=0)
    q_big = jnp.where(own_head, q_rep, 0.0).astype(BF16)
    pad = jnp.zeros((QB - DEC_SEQ, width), F32)
    kn = jnp.concatenate([wide(K), pad], axis=0).astype(BF16)
    vn = jnp.concatenate([wide(V), pad], axis=0).astype(BF16)
    cc = cc_ref[...]
    cn = cn_ref[...]
    kt = kt_ref[...].reshape(width, CACHE_LEN).astype(BF16)
    s_c = jnp.where(cc > 0.0, _dot(q_big, kt), NEG)
    s_n = jnp.where(cn > 0.0, _dot_nt(q_big, kn), NEG)
    m = jnp.maximum(jnp.max(s_c, axis=1, keepdims=True), jnp.max(s_n, axis=1, keepdims=True))
    e_c = jnp.exp(s_c - m) * cc
    e_n = jnp.exp(s_n - m) * cn
    l = jnp.sum(e_c, axis=1, keepdims=True) + jnp.sum(e_n, axis=1, keepdims=True)
    vt = vt_ref[...].reshape(width, CACHE_LEN).astype(BF16)
    out_big = (_dot_nt(e_c.astype(BF16), vt) + _dot(e_n.astype(BF16), vn)) / l
    out_big = jnp.where(own_head, out_big, 0.0)
    out = jnp.sum(out_big.reshape(HEADS_PER_STEP, DEC_SEQ, width), axis=0)
    o_ref[...] = (out * wide(Z)).astype(o_ref.dtype)


def _attn_kernel(xs_ref, kt_ref, vt_ref, cc_ref, cn_ref, xp_ref, bias_ref, os_ref, op_ref, *scratch):
    _sample_attention_step(xs_ref, kt_ref, vt_ref, cc_ref, cn_ref, os_ref)
    _prompt_attention_step(pl.program_id(0) % STEPS_PER_UNIT, xp_ref, bias_ref, op_ref, *scratch)


def _attn(x, kt_all, vt_all, e, cc, cn, bias):
    first = ROWS_P // DEC_SEQ
    width = HEADS_PER_STEP * HD
    per_row = H_B // HEADS_PER_STEP
    n_steps = DEC_BATCH * per_row
    assert n_steps == BATCH * N_PAIRS * STEPS_PER_UNIT
    n_q = HEADS_PER_STEP * DEC_SEQ
    once = pl.Buffered(1)
    unit = lambda s: s // STEPS_PER_UNIT
    cache_spec = pl.BlockSpec((None, None, HEADS_PER_STEP, HD, CACHE_LEN),
                              lambda s: (e, s // per_row, s % per_row, 0, 0))
    return pl.pallas_call(
        _attn_kernel,
        grid=(n_steps,),
        in_specs=[pl.BlockSpec((PAIRS_PER_STEP, 4, DEC_SEQ, LANES), lambda s: (s % per_row, 0, first + s // per_row, 0)),
                  cache_spec, cache_spec,
                  pl.BlockSpec((n_q, CACHE_LEN), lambda s: (0, 0), pipeline_mode=once),
                  pl.BlockSpec((n_q, QB), lambda s: (0, 0), pipeline_mode=once),
                  pl.BlockSpec((None, 4, SEQ, LANES), lambda s: (unit(s) % N_PAIRS, 0, unit(s) // N_PAIRS, 0)),
                  pl.BlockSpec((2, QB, 2 * QB), lambda s: (0, 0, 0), pipeline_mode=once)],
        out_specs=[pl.BlockSpec((DEC_SEQ, width), lambda s: (s // per_row, s % per_row)),
                   pl.BlockSpec((SEQ, LANES), lambda s: (unit(s) // N_PAIRS, unit(s) % N_PAIRS))],
        out_shape=[jax.ShapeDtypeStruct((ROWS_S, W_B), BF16),
                   jax.ShapeDtypeStruct((ROWS_P, W_B), BF16)],
        scratch_shapes=[pltpu.VMEM((SEQ, LANES), BF16)] * 2
                       + [pltpu.VMEM((QB + SEQ, LANES), BF16)] * 2
                       + [pltpu.VMEM((SEQ, LANES), F32)] * 6,
        compiler_params=pltpu.CompilerParams(dimension_semantics=("arbitrary",),
                                             vmem_limit_bytes=ATTN_VMEM_LIMIT),
        name="attn",
    )(x, kt_all, vt_all, cc, cn, x, bias)


def _row_tiles(x, width, col=0):
    if not isinstance(x, tuple):
        return [pl.BlockSpec((TM, width), lambda i: (i, col))], [x]
    x_p, x_s = x
    return ([pl.BlockSpec((TM, width), lambda i: (jnp.minimum(i, N_TILES_P - 1), col)),
             pl.BlockSpec((TM, width), lambda i: (jnp.maximum(i - N_TILES_P, 0), col))],
            [x_p, x_s])


def _read_tile(refs):
    if len(refs) == 1:
        return refs[0][...]
    return jnp.where(pl.program_id(0) < N_TILES_P, refs[0][...], refs[1][...])


def _out_proj_kernel(n_b, n_h, final, a_ref, *refs):
    b_refs, refs = refs[:n_b], refs[n_b:]
    w_ref, refs = refs[0], refs[1:]
    h_refs, (g_ref, *out_refs) = refs[:n_h], refs[n_h:]
    half = w_ref.shape[0] // 2
    h_new = (_read_tile(h_refs) + _dot(a_ref[...], w_ref[0:half, :])
             + _dot(_read_tile(b_refs), w_ref[half:, :]))
    ms = jnp.mean(h_new * h_new, axis=-1, keepdims=True)
    normed = h_new * lax.rsqrt(ms + EPS) * g_ref[...]
    if not final:
        h_out_ref, n_out_ref = out_refs
        h_out_ref[...] = h_new
        n_out_ref[...] = normed.astype(n_out_ref.dtype)
        return
    y_p_ref, y_s_ref = out_refs
    y_s_ref[...] = normed

    @pl.when(pl.program_id(0) < N_TILES_P)
    def _():
        y_p_ref[...] = normed


def _out_proj(a, a_col, b, b_col, w, e, h, g, final):
    half = w.shape[1] // 2
    b_specs, b_args = _row_tiles(b, half, b_col)
    h_specs, h_args = _row_tiles(h, D_MODEL)
    if final:
        out_specs = [pl.BlockSpec((TM, D_MODEL), lambda i: (jnp.minimum(i, N_TILES_P - 1), 0)),
                     pl.BlockSpec((TM, D_MODEL), lambda i: (jnp.maximum(i - N_TILES_P, 0), 0))]
        out_shape = [jax.ShapeDtypeStruct((ROWS_P, D_MODEL), F32), jax.ShapeDtypeStruct((ROWS_S, D_MODEL), F32)]
    else:
        out_specs = [pl.BlockSpec((TM, D_MODEL), lambda i: (i, 0))] * 2
        out_shape = [jax.ShapeDtypeStruct((ROWS, D_MODEL), F32), jax.ShapeDtypeStruct((ROWS, D_MODEL), BF16)]
    return pl.pallas_call(
        functools.partial(_out_proj_kernel, len(b_args), len(h_args), final),
        grid=(N_TILES,),
        in_specs=[pl.BlockSpec((TM, half), lambda i: (i, a_col))] + b_specs
                 + [pl.BlockSpec((None, 2 * half, D_MODEL), lambda i: (e, 0, 0))] + h_specs
                 + [pl.BlockSpec((1, D_MODEL), lambda i: (0, 0))],
        out_specs=out_specs,
        out_shape=out_shape,
        compiler_params=_params("arbitrary"),
        name="out_proj",
    )(a, *b_args, w, *h_args, g.reshape(1, D_MODEL))


TM_O = 256
N_TILES_O = ROWS // TM_O
N_TILES_O_P = ROWS_P // TM_O
TILES_PER_SEQ_O = SEQ // TM_O
TN_ODD = 512
PRE_SLOTS = BATCH + ROWS_S // TM_O


def _odd_in_kernel(x_ref, w_ref, cw_ref, e1_ref, e2_ref, y_ref, pre_ref, buf):
    i = pl.program_id(0)

    @pl.when(i == 0)
    def _():
        buf[0:SUBLANES, :] = jnp.zeros((SUBLANES, W_C), F32)

    x = x_ref[...]
    row = lax.broadcasted_iota(jnp.int32, (TM_O, TN_ODD), 0)
    t = jnp.where(i < N_TILES_O_P, (i % TILES_PER_SEQ_O) * TM_O + row, row % DEC_SEQ)
    is_sample = i >= N_TILES_O_P
    for c in range(W_C // TN_ODD):
        cols = slice(c * TN_ODD, (c + 1) * TN_ODD)
        part = lambda k: w_ref[:, k * W_C + c * TN_ODD:k * W_C + (c + 1) * TN_ODD]
        pre = _dot(x, part(1)) * _dot(x, part(2))
        pre_ref[:, cols] = pre
        buf[SUBLANES:SUBLANES + TM_O, cols] = pre
        pre_m1 = jnp.where(t >= 1, buf[SUBLANES - 1:SUBLANES - 1 + TM_O, cols],
                           jnp.where(is_sample, e1_ref[:, cols], 0.0))
        pre_m2 = jnp.where(t >= 2, buf[SUBLANES - 2:SUBLANES - 2 + TM_O, cols],
                           jnp.where(is_sample, e2_ref[:, cols], 0.0))
        conv = cw_ref[0:1, cols] * pre_m2 + cw_ref[1:2, cols] * pre_m1 + cw_ref[2:3, cols] * pre
        y = _dot(x, part(0)) * conv * _silu(_dot(x, part(3)))
        y_ref[:, cols] = y.astype(y_ref.dtype)
        buf[0:SUBLANES, cols] = pre[TM_O - SUBLANES:, :]


def _odd_in(n, w, layer, conv_w, e1, e2):
    e_block = lambda i: (jnp.maximum(i - N_TILES_O_P, 0), 0)
    pre_block = lambda i: (jnp.where(i < N_TILES_O_P, i // TILES_PER_SEQ_O, i - N_TILES_O_P + BATCH), 0)
    return pl.pallas_call(
        _odd_in_kernel,
        grid=(N_TILES_O,),
        in_specs=[pl.BlockSpec((TM_O, D_MODEL), lambda i: (i, 0)),
                  pl.BlockSpec((None, D_MODEL, ODD_IN), lambda i: (layer, 0, 0), pipeline_mode=pl.Buffered(1)),
                  pl.BlockSpec((CONV_W, W_C), lambda i: (0, 0)),
                  pl.BlockSpec((TM_O, W_C), e_block),
                  pl.BlockSpec((TM_O, W_C), e_block)],
        out_specs=[pl.BlockSpec((TM_O, W_C), lambda i: (i, 0)),
                   pl.BlockSpec((TM_O, W_C), pre_block)],
        out_shape=[jax.ShapeDtypeStruct((ROWS, W_C), BF16),
                   jax.ShapeDtypeStruct((PRE_SLOTS * TM_O, W_C), F32)],
        scratch_shapes=[pltpu.VMEM((SUBLANES + TM_O, W_C), F32)],
        compiler_params=_params("arbitrary"),
        name="odd_in",
    )(n, w, conv_w, e1, e2)


def _rope_tables():
    half = HD // 2
    inv = 1.0 / (ROPE_THETA ** (jnp.arange(half, dtype=F32) / half))
    pos_s = PAST_LEN + (jnp.arange(TM_E) % DEC_SEQ)
    pos = jnp.concatenate([jnp.arange(SEQ), pos_s]).astype(F32)
    ang = pos[:, None] * inv[None, :]
    cos = jnp.cos(ang)
    sin = jnp.sin(ang)
    reps = LANES // HD
    return (jnp.tile(jnp.concatenate([cos, cos], axis=1), (1, reps)),
            jnp.tile(jnp.concatenate([-sin, sin], axis=1), (1, reps)))


def _gmlp_weights(w_s, b_s):
    tril = jnp.tril(jnp.ones((CHUNK, CHUNK), bool))
    w_p = jnp.where(tril[None], w_s, 0.0)
    small = w_p[:, :DEC_SEQ, :DEC_SEQ]
    eye = jnp.eye(CHUNK // DEC_SEQ, dtype=F32)
    w_smp = jnp.einsum('ab,hts->hatbs', eye, small).reshape(H_A, CHUNK, CHUNK)
    wm = jnp.stack([w_p, w_smp]).astype(BF16)
    bias_p = jnp.repeat(b_s.T, CH_A, axis=1)
    bias_s = jnp.tile(bias_p[:DEC_SEQ], (CHUNK // DEC_SEQ, 1))
    return wm, jnp.stack([bias_p, bias_s])


def _conv_edges(state):
    e1 = jnp.pad(state[:, CONV_W - 2:], ((0, 0), (0, DEC_SEQ - 1), (0, 0)))
    e2 = jnp.pad(state, ((0, 0), (0, DEC_SEQ - (CONV_W - 1)), (0, 0)))
    return e1.reshape(ROWS_S, W_C), e2.reshape(ROWS_S, W_C)


def _prompt_cache_rows(t_all):
    return jnp.transpose(t_all.reshape(N_EVEN, BATCH, H_B, HD, t_all.shape[-1]), (0, 1, 4, 2, 3))


def kernel(x_prompt, x_sample, cache_b_k, cache_b_v, state_c_conv, norm_w, final_norm_w,
           w_in_even, w_s, b_s, w_out_even, w_in_odd, conv_w, w_out_odd):
    h = (x_prompt.reshape(ROWS_P, D_MODEL), x_sample.reshape(ROWS_S, D_MODEL))
    cos, sin = _rope_tables()
    band_bias = jnp.asarray(_band_bias())
    cc, cn = (jnp.asarray(c) for c in _sample_key_counts())
    kt_all = jnp.transpose(cache_b_k, (0, 1, 3, 4, 2))
    vt_all = jnp.transpose(cache_b_v, (0, 1, 3, 4, 2))
    w_in_even, w_out_even, w_in_odd, w_out_odd = (w.astype(BF16) for w in (w_in_even, w_out_even, w_in_odd, w_out_odd))
    n = _rmsnorm(h, norm_w[0])
    ks_l, vs_l, av_l, cp_l, cs_l = [], [], [], [], []
    new_kt = new_vt = None
    for layer in range(DEPTH):
        last = layer == DEPTH - 1
        g_next = final_norm_w if last else norm_w[layer + 1]
        if layer % 2 == 0:
            e = layer // 2
            wm, bias = _gmlp_weights(w_s[e], b_s[e])
            a_out, av, att_in, kv, kt_e, vt_e = _even_in(n, w_in_even, cos, sin, wm, bias, e, None, None)
            new_kt = [kt_e] if new_kt is None else new_kt + [kt_e]
            new_vt = [vt_e] if new_vt is None else new_vt + [vt_e]
            att_s, att_p = _attn(att_in, kt_all, vt_all, e, cc, cn, band_bias)
            h, n = _out_proj(a_out, 0, (att_p, att_s), 0, w_out_even, e, h, g_next, last)
            ks_l.append(kv[:, :W_B].reshape(DEC_BATCH, DEC_SEQ, H_B, HD))
            vs_l.append(kv[:, W_B:].reshape(DEC_BATCH, DEC_SEQ, H_B, HD))
            av_l.append(av)
        else:
            c = layer // 2
            e1, e2 = _conv_edges(state_c_conv[c])
            y, pre = _odd_in(n, w_in_odd, c, conv_w[c], e1, e2)
            h, n = _out_proj(y, 0, y, 1, w_out_odd, c, h, g_next, last)
            pre_p = pre[:BATCH * TM_O].reshape(BATCH, TM_O, W_C)
            cp_l.append(pre_p[:, TM_O - (CONV_W - 1):])
            cs_l.append(pre[BATCH * TM_O:].reshape(DEC_BATCH, DEC_SEQ, W_C)[:, DEC_SEQ - (CONV_W - 1):])
    y_prompt = h.reshape(BATCH, SEQ, D_MODEL)
    y_sample = n.reshape(DEC_BATCH, DEC_SEQ, D_MODEL)
    new_a_v_sample = jnp.stack(av_l).reshape(N_EVEN, DEC_BATCH, DEC_SEQ, W_A)
    new_kt, new_vt = jnp.stack(new_kt), jnp.stack(new_vt)
    return (y_prompt, y_sample, _prompt_cache_rows(new_kt), _prompt_cache_rows(new_vt), jnp.stack(ks_l), jnp.stack(vs_l),
            new_a_v_sample, jnp.stack(cp_l), jnp.stack(cs_l))
```
